```python
import math
import jax
import jax.numpy as jnp
from jax import lax
import numpy as np

D_MODEL = 1024
BATCH = 8
SEQ = 8192
DEPTH = 4

N_MEM = 256
HEAD_DIM = 64
ROPE_THETA = 10000.0
RMS_EPS = 1e-6
NEG_INF = -1e30
Q_BLOCK = 128

SB_HEADS = 8
NSA_HEADS = 8
NSA_KV_GROUPS = 2
CMP_LEN = 32
CMP_STRIDE = 16
CMP_HIDDEN = 128
SEL_BLOCK = 64
SEL_TOPN = 16
WINDOW = 512
FORCE_BONUS = 1e4
N_SUB = Q_BLOCK // SEL_BLOCK
SB_W = SB_HEADS * HEAD_DIM
NSA_QW = NSA_HEADS * HEAD_DIM
NSA_KVW = NSA_KV_GROUPS * HEAD_DIM
N_GATES = 3 * NSA_HEADS
HYB_IN = 3 * SB_W + NSA_QW + 6 * NSA_KVW + N_GATES
HYB_MIX = SB_W + NSA_QW

SSD_INNER = 2 * D_MODEL
SSD_HEADDIM = 64
SSD_HEADS = SSD_INNER // SSD_HEADDIM
SSD_GROUPS = 4
SSD_STATE = 128
SSD_CONV = 4
SSD_CHUNK = 128
SSD_CONV_CH = SSD_INNER + 2 * SSD_GROUPS * SSD_STATE
SSD_IN = SSD_INNER + SSD_CONV_CH + SSD_HEADS

D_FF = 3 * D_MODEL // 2
CROSS_HEADS = 4
CROSS_HD = D_MODEL // CROSS_HEADS
N_NORMS = 9

kernel_name = 'hybrid_sb_nsa_ssd_trunk'


def split_cols(t, sizes):
    offs = []
    acc = 0
    for s in sizes[:-1]:
        acc += s
        offs.append(acc)
    return jnp.split(t, offs, axis=-1)


def rmsnorm(x, g):
    xf = x.astype(jnp.float32)
    y = xf * lax.rsqrt(jnp.mean(xf * xf, axis=-1, keepdims=True) + RMS_EPS)
    return (y * g.astype(jnp.float32)).astype(x.dtype)


def rotary(x, positions):
    half = x.shape[-1] // 2
    inv = ROPE_THETA ** (-jnp.arange(half, dtype=jnp.float32) / half)
    ang = positions.astype(jnp.float32)[..., None] * inv
    cos = jnp.cos(ang)[:, :, None, :]
    sin = jnp.sin(ang)[:, :, None, :]
    x1 = x[..., :half].astype(jnp.float32)
    x2 = x[..., half:].astype(jnp.float32)
    return jnp.concatenate([x1 * cos - x2 * sin, x2 * cos + x1 * sin], axis=-1).astype(x.dtype)


def swiglu(x, w_gu, w_down):
    gate, up = jnp.split(x @ w_gu, 2, axis=-1)
    return (jax.nn.silu(gate) * up) @ w_down


def masked_softmax(s, mask):
    return jax.nn.softmax(jnp.where(mask, s, NEG_INF), axis=-1)


def stick_breaking_attention(q, k, v):
    B, S, H, d = q.shape
    nb = S // Q_BLOCK
    scale = d ** -0.5
    f32 = jnp.float32
    qb = q.reshape(B, nb, Q_BLOCK, H, d).swapaxes(0, 1)
    kb = k.reshape(B, nb, Q_BLOCK, H, d)
    vb = v.reshape(B, nb, Q_BLOCK, H, d)
    ar = jnp.arange(Q_BLOCK)
    strict = ar[None, :] < ar[:, None]

    def block(args):
        qi, i = args

        def body(step, carry):
            acc, log_carry = carry
            j = i - step
            kj = lax.dynamic_index_in_dim(kb, j, axis=1, keepdims=False)
            vj = lax.dynamic_index_in_dim(vb, j, axis=1, keepdims=False)
            z = jnp.einsum('bqhd,bkhd->bhqk', qi, kj).astype(f32) * scale
            mask = jnp.where(j == i, strict, True)
            ls = jnp.where(mask, jax.nn.log_sigmoid(-z), 0.0)
            after = lax.cumsum(ls, axis=3, reverse=True) - ls + log_carry[..., None]
            a = jnp.where(mask, jnp.exp(jax.nn.log_sigmoid(z) + after), 0.0)
            acc = acc + jnp.einsum('bhqk,bkhd->bqhd', a.astype(vj.dtype), vj).astype(f32)
            return acc, log_carry + jnp.sum(ls, axis=-1)

        acc0 = jnp.zeros((B, Q_BLOCK, H, d), f32)
        lc0 = jnp.zeros((B, H, Q_BLOCK), f32)
        acc, _ = lax.fori_loop(0, i + 1, body, (acc0, lc0))
        return acc.astype(q.dtype)

    out = lax.map(block, (qb, jnp.arange(nb)))
    return out.swapaxes(0, 1).reshape(B, S, H, d)


def compress_blocks(t, pe, w1, w2):
    B, S, G, d = t.shape
    n_c = (S - CMP_LEN) // CMP_STRIDE + 1
    idx = jnp.arange(n_c)[:, None] * CMP_STRIDE + jnp.arange(CMP_LEN)[None, :]
    blk = t[:, idx] + pe[:, None, :]
    blk = blk.transpose(0, 1, 3, 2, 4).reshape(B, n_c, G, CMP_LEN * d)
    return jax.nn.silu(blk @ w1) @ w2


def nsa_attention(q, k_cmp, v_cmp, k_sel, v_sel, k_win, v_win, gates,
                  pe_k, w1_k, w2_k, pe_v, w1_v, w2_v):
    B, S, H, d = q.shape
    G = k_cmp.shape[2]
    R = H // G
    nb = S // Q_BLOCK
    scale = d ** -0.5
    f32 = jnp.float32
    kc = compress_blocks(k_cmp, pe_k, w1_k, w2_k)
    vc = compress_blocks(v_cmp, pe_v, w1_v, w2_v)
    n_c = kc.shape[1]
    cmp_start = jnp.arange(n_c) * CMP_STRIDE
    cmp_end = cmp_start + CMP_LEN - 1
    n_sel = S // SEL_BLOCK
    sel_start = jnp.arange(n_sel) * SEL_BLOCK
    overlap = ((cmp_start[:, None] < sel_start[None, :] + SEL_BLOCK)
               & (cmp_end[:, None] >= sel_start[None, :])).astype(f32)
    n_top = min(SEL_TOPN, n_sel)
    ksb = k_sel.reshape(B, n_sel, SEL_BLOCK, G, d).transpose(0, 3, 1, 2, 4)
    vsb = v_sel.reshape(B, n_sel, SEL_BLOCK, G, d).transpose(0, 3, 1, 2, 4)
    kwp = jnp.pad(k_win, ((0, 0), (WINDOW, 0), (0, 0), (0, 0)))
    vwp = jnp.pad(v_win, ((0, 0), (WINDOW, 0), (0, 0), (0, 0)))
    b_ix = jnp.arange(B)[:, None, None, None]
    g_ix = jnp.arange(G)[None, :, None, None]
    blk_tok = jnp.arange(SEL_BLOCK)
    win_off = jnp.arange(WINDOW + Q_BLOCK) - WINDOW
    blk_ids = jnp.arange(n_sel)
    qb = q.reshape(B, nb, Q_BLOCK, H, d).swapaxes(0, 1)
    gb = gates.reshape(B, nb, Q_BLOCK, H, 3).swapaxes(0, 1)

    def block(args):
        qi, gi, i = args
        q_pos = i * Q_BLOCK + jnp.arange(Q_BLOCK)
        qg = qi.reshape(B, Q_BLOCK, G, R, d)
        s_c = jnp.einsum('bqgrd,bcgd->bgrqc', qg, kc).astype(f32) * scale
        mask_c = cmp_end[None, :] <= q_pos[:, None]
        p_c = masked_softmax(s_c, mask_c) * mask_c
        o_c = jnp.einsum('bgrqc,bcgd->bqgrd', p_c.astype(vc.dtype), vc)
        sub_start = i * Q_BLOCK + jnp.arange(N_SUB) * SEL_BLOCK
        p_first = p_c[:, :, :, ::SEL_BLOCK]
        imp = jnp.einsum('bgrnc,cj->bgnj', p_first, overlap)
        cur = sub_start // SEL_BLOCK
        valid = sel_start[None, :] <= sub_start[:, None]
        forced = ((blk_ids[None, :] == 0) | (blk_ids[None, :] == cur[:, None])
                  | (blk_ids[None, :] == cur[:, None] - 1))
        score = jnp.where(valid, imp + jnp.where(forced, FORCE_BONUS, 0.0), -1.0)
        _, sel_idx = lax.top_k(score, n_top)
        k_g = ksb[b_ix, g_ix, sel_idx].reshape(B, G, N_SUB, n_top * SEL_BLOCK, d)
        v_g = vsb[b_ix, g_ix, sel_idx].reshape(B, G, N_SUB, n_top * SEL_BLOCK, d)
        tok_pos = (sel_idx[..., None] * SEL_BLOCK + blk_tok).reshape(B, G, N_SUB, n_top * SEL_BLOCK)
        q_pos_sub = q_pos.reshape(N_SUB, SEL_BLOCK)
        mask_s = tok_pos[:, :, None, :, None, :] <= q_pos_sub[None, None, None, :, :, None]
        qs = qi.reshape(B, N_SUB, SEL_BLOCK, G, R, d)
        s_s = jnp.einsum('bnqgrd,bgntd->bgrnqt', qs, k_g).astype(f32) * scale
        p_s = masked_softmax(s_s, mask_s)
        o_s = jnp.einsum('bgrnqt,bgntd->bnqgrd', p_s.astype(v_g.dtype), v_g).reshape(B, Q_BLOCK, G, R, d)
        kw = lax.dynamic_slice_in_dim(kwp, i * Q_BLOCK, WINDOW + Q_BLOCK, axis=1)
        vw = lax.dynamic_slice_in_dim(vwp, i * Q_BLOCK, WINDOW + Q_BLOCK, axis=1)
        k_pos = i * Q_BLOCK + win_off
        delta = q_pos[:, None] - k_pos[None, :]
        mask_w = (delta >= 0) & (delta < WINDOW) & (k_pos[None, :] >= 0)
        s_w = jnp.einsum('bqgrd,bkgd->bgrqk', qg, kw).astype(f32) * scale
        p_w = masked_softmax(s_w, mask_w)
        o_w = jnp.einsum('bgrqk,bkgd->bqgrd', p_w.astype(vw.dtype), vw)
        g = jax.nn.sigmoid(gi.astype(f32)).reshape(B, Q_BLOCK, G, R, 3)
        o = g[..., 0:1] * o_c + g[..., 1:2] * o_s + g[..., 2:3] * o_w
        return o.reshape(B, Q_BLOCK, H, d).astype(q.dtype)

    out = lax.map(block, (qb, gb, jnp.arange(nb)))
    return out.swapaxes(0, 1).reshape(B, S, H, d)


def hybrid_attention_mixer(u, positions, w_in, w_out, pe_k, w1_k, w2_k, pe_v, w1_v, w2_v):
    B, S, _ = u.shape
    parts = split_cols(u @ w_in, [SB_W, SB_W, SB_W, NSA_QW] + [NSA_KVW] * 6 + [N_GATES])
    q_sb, k_sb, v_sb = [t.reshape(B, S, SB_HEADS, HEAD_DIM) for t in parts[:3]]
    q_n = rotary(parts[3].reshape(B, S, NSA_HEADS, HEAD_DIM), positions)
    k_cmp, v_cmp, k_sel, v_sel, k_win, v_win = [t.reshape(B, S, NSA_KV_GROUPS, HEAD_DIM) for t in parts[4:10]]
    k_cmp = rotary(k_cmp, positions)
    k_sel = rotary(k_sel, positions)
    k_win = rotary(k_win, positions)
    gates = parts[10].reshape(B, S, NSA_HEADS, 3)
    o_sb = stick_breaking_attention(q_sb, k_sb, v_sb).reshape(B, S, SB_W)
    o_nsa = nsa_attention(q_n, k_cmp, v_cmp, k_sel, v_sel, k_win, v_win, gates,
                          pe_k, w1_k, w2_k, pe_v, w1_v, w2_v).reshape(B, S, NSA_QW)
    return jnp.concatenate([o_sb, o_nsa], axis=-1) @ w_out


def causal_depthwise_conv(x, w, b):
    C = x.shape[-1]
    y = lax.conv_general_dilated(x, w[:, None, :], window_strides=(1,), padding=[(SSD_CONV - 1, 0)],
                                 dimension_numbers=('NWC', 'WIO', 'NWC'), feature_group_count=C)
    return y + b


def ssd_chunked_scan(x, dt, a, b_in, c_in):
    Bsz, S, G, R, P = x.shape
    N = b_in.shape[-1]
    L = SSD_CHUNK
    nc = S // L

    def chunks(t):
        return t.reshape(Bsz, nc, L, *t.shape[2:]).swapaxes(0, 1)

    causal = jnp.tril(jnp.ones((L, L), dtype=bool))[None, :, :, None, None]

    def step(state, inp):
        x_c, dt_c, b_c, c_c = inp
        acum = jnp.cumsum(dt_c * a, axis=1)
        seg = acum[:, :, None] - acum[:, None, :]
        decay = jnp.exp(jnp.where(causal, seg, NEG_INF))
        cb = jnp.einsum('btgn,bsgn->btsg', c_c, b_c)
        w = cb[..., None] * decay * dt_c[:, None]
        y_diag = jnp.einsum('btsgr,bsgrp->btgrp', w, x_c)
        y_off = jnp.einsum('btgn,bgrpn->btgrp', c_c, state) * jnp.exp(acum)[..., None]
        decay_end = jnp.exp(acum[:, -1:] - acum) * dt_c
        new_state = (state * jnp.exp(acum[:, -1])[..., None, None]
                     + jnp.einsum('bsgn,bsgrp->bgrpn', b_c, decay_end[..., None] * x_c))
        return new_state, y_diag + y_off

    state0 = jnp.zeros((Bsz, G, R, P, N), jnp.float32)
    _, ys = lax.scan(step, state0, (chunks(x), chunks(dt), chunks(b_in), chunks(c_in)))
    return ys.swapaxes(0, 1).reshape(Bsz, S, G, R, P)


def ssd_mixer(u, w_in, conv_w, conv_b, dt_bias, a_log, d_skip, norm_g, w_out):
    B, S, _ = u.shape
    G, R, P, N = SSD_GROUPS, SSD_HEADS // SSD_GROUPS, SSD_HEADDIM, SSD_STATE
    f32 = jnp.float32
    z, xbc, dt = split_cols(u @ w_in, [SSD_INNER, SSD_CONV_CH, SSD_HEADS])
    xbc = jax.nn.silu(causal_depthwise_conv(xbc, conv_w, conv_b))
    xs, b_in, c_in = split_cols(xbc, [SSD_INNER, G * N, G * N])
    dt = jax.nn.softplus(dt.astype(f32) + dt_bias.astype(f32)).reshape(B, S, G, R)
    a = -jnp.exp(a_log.astype(f32)).reshape(G, R)
    xs = xs.astype(f32).reshape(B, S, G, R, P)
    y = ssd_chunked_scan(xs, dt, a, b_in.astype(f32).reshape(B, S, G, N), c_in.astype(f32).reshape(B, S, G, N))
    y = y + d_skip.astype(f32).reshape(G, R)[:, :, None] * xs
    y = y.reshape(B, S, SSD_INNER) * jax.nn.silu(z.astype(f32))
    y = rmsnorm(y, norm_g).astype(u.dtype)
    return y @ w_out


def memory_cross_attention(h, mem_n, wq, wkv, wo):
    B, S, _ = h.shape
    M = mem_n.shape[1]
    q = (h @ wq).reshape(B, S, CROSS_HEADS, CROSS_HD)
    k, v = jnp.split(mem_n @ wkv, 2, axis=-1)
    k = k.reshape(B, M, CROSS_HEADS, CROSS_HD)
    v = v.reshape(B, M, CROSS_HEADS, CROSS_HD)
    s = jnp.einsum('bqhd,bmhd->bhqm', q, k).astype(jnp.float32) * CROSS_HD ** -0.5
    p = jax.nn.softmax(s, axis=-1)
    o = jnp.einsum('bhqm,bmhd->bqhd', p.astype(v.dtype), v).reshape(B, S, D_MODEL)
    return o @ wo


def setup_inputs(seed: int = 0) -> dict:
    key = jax.random.key(seed)
    keys = iter(jax.random.split(key, 48))
    f32 = jnp.float32
    n_even = (DEPTH + 1) // 2
    n_odd = DEPTH // 2

    def dense(shape, fan_in):
        return jax.random.normal(next(keys), shape, f32) * fan_in ** -0.5

    def gain(shape):
        return 1.0 + 0.02 * jax.random.normal(next(keys), shape, f32)

    x = jax.random.normal(next(keys), (BATCH, SEQ, D_MODEL), f32)
    mem = jax.random.normal(next(keys), (BATCH, N_MEM, D_MODEL), f32)
    positions = (jnp.arange(SEQ, dtype=jnp.int32)[None, :]
                 + jax.random.randint(next(keys), (BATCH, 1), 0, 4096, dtype=jnp.int32))
    norm_g = gain((DEPTH, N_NORMS, D_MODEL))
    ffn1_w_gu = dense((DEPTH, D_MODEL, 2 * D_FF), D_MODEL)
    ffn1_w_down = dense((DEPTH, D_FF, D_MODEL), D_FF)
    ffn2_w_gu = dense((DEPTH, D_MODEL, 2 * D_FF), D_MODEL)
    ffn2_w_down = dense((DEPTH, D_FF, D_MODEL), D_FF)
    cross_wq = dense((DEPTH, D_MODEL, D_MODEL), D_MODEL)
    cross_wkv = dense((DEPTH, D_MODEL, 2 * D_MODEL), D_MODEL)
    cross_wo = dense((DEPTH, D_MODEL, D_MODEL), D_MODEL)
    hyb_w_in = dense((n_even, D_MODEL, HYB_IN), D_MODEL)
    hyb_w_out = dense((n_even, HYB_MIX, D_MODEL), HYB_MIX)
    cmp_pe_k = 0.1 * jax.random.normal(next(keys), (n_even, CMP_LEN, HEAD_DIM), f32)
    cmp_w1_k = dense((n_even, CMP_LEN * HEAD_DIM, CMP_HIDDEN), CMP_LEN * HEAD_DIM)
    cmp_w2_k = dense((n_even, CMP_HIDDEN, HEAD_DIM), CMP_HIDDEN)
    cmp_pe_v = 0.1 * jax.random.normal(next(keys), (n_even, CMP_LEN, HEAD_DIM), f32)
    cmp_w1_v = dense((n_even, CMP_LEN * HEAD_DIM, CMP_HIDDEN), CMP_LEN * HEAD_DIM)
    cmp_w2_v = dense((n_even, CMP_HIDDEN, HEAD_DIM), CMP_HIDDEN)
    ssd_w_in = dense((n_odd, D_MODEL, SSD_IN), D_MODEL)
    ssd_conv_w = dense((n_odd, SSD_CONV, SSD_CONV_CH), SSD_CONV)
    ssd_conv_b = 0.02 * jax.random.normal(next(keys), (n_odd, SSD_CONV_CH), f32)
    dt0 = jnp.exp(jax.random.uniform(next(keys), (n_odd, SSD_HEADS), f32)
                  * (math.log(0.1) - math.log(0.001)) + math.log(0.001))
    ssd_dt_bias = dt0 + jnp.log(-jnp.expm1(-dt0))
    ssd_A_log = jnp.log(jax.random.uniform(next(keys), (n_odd, SSD_HEADS), f32, minval=1.0, maxval=16.0))
    ssd_D = 1.0 + 0.1 * jax.random.normal(next(keys), (n_odd, SSD_HEADS), f32)
    ssd_norm_g = gain((n_odd, SSD_INNER))
    ssd_w_out = dense((n_odd, SSD_INNER, D_MODEL), SSD_INNER)
    return {'x': x, 'mem': mem, 'positions': positions, 'norm_g': norm_g,
            'ffn1_w_gu': ffn1_w_gu, 'ffn1_w_down': ffn1_w_down,
            'ffn2_w_gu': ffn2_w_gu, 'ffn2_w_down': ffn2_w_down,
            'cross_wq': cross_wq, 'cross_wkv': cross_wkv, 'cross_wo': cross_wo,
            'hyb_w_in': hyb_w_in, 'hyb_w_out': hyb_w_out,
            'cmp_pe_k': cmp_pe_k, 'cmp_w1_k': cmp_w1_k, 'cmp_w2_k': cmp_w2_k,
            'cmp_pe_v': cmp_pe_v, 'cmp_w1_v': cmp_w1_v, 'cmp_w2_v': cmp_w2_v,
            'ssd_w_in': ssd_w_in, 'ssd_conv_w': ssd_conv_w, 'ssd_conv_b': ssd_conv_b,
            'ssd_dt_bias': ssd_dt_bias, 'ssd_A_log': ssd_A_log, 'ssd_D': ssd_D,
            'ssd_norm_g': ssd_norm_g, 'ssd_w_out': ssd_w_out}


def reference(x, mem, positions, norm_g, ffn1_w_gu, ffn1_w_down, ffn2_w_gu, ffn2_w_down,
              cross_wq, cross_wkv, cross_wo, hyb_w_in, hyb_w_out,
              cmp_pe_k, cmp_w1_k, cmp_w2_k, cmp_pe_v, cmp_w1_v, cmp_w2_v,
              ssd_w_in, ssd_conv_w, ssd_conv_b, ssd_dt_bias, ssd_A_log, ssd_D, ssd_norm_g, ssd_w_out):
    h = x
    for i in range(DEPTH):
        g = norm_g[i]
        j = i // 2
        h = h + 0.5 * rmsnorm(swiglu(rmsnorm(h, g[0]), ffn1_w_gu[i], ffn1_w_down[i]), g[1])
        u = rmsnorm(h, g[2])
        if i % 2 == 0:
            m = hybrid_attention_mixer(u, positions, hyb_w_in[j], hyb_w_out[j],
                                       cmp_pe_k[j], cmp_w1_k[j], cmp_w2_k[j],
                                       cmp_pe_v[j], cmp_w1_v[j], cmp_w2_v[j])
        else:
            m = ssd_mixer(u, ssd_w_in[j], ssd_conv_w[j], ssd_conv_b[j], ssd_dt_bias[j],
                          ssd_A_log[j], ssd_D[j], ssd_norm_g[j], ssd_w_out[j])
        h = h + rmsnorm(m, g[3])
        c = memory_cross_attention(rmsnorm(h, g[4]), rmsnorm(mem, g[6]), cross_wq[i], cross_wkv[i], cross_wo[i])
        h = h + rmsnorm(c, g[5])
        h = h + 0.5 * rmsnorm(swiglu(rmsnorm(h, g[7]), ffn2_w_gu[i], ffn2_w_down[i]), g[8])
    return h
```

```python
import functools
import math

import jax
import jax.numpy as jnp
import numpy as np
from jax import lax
from jax.experimental import pallas as pl
from jax.experimental.pallas import tpu as pltpu

F32 = jnp.float32
BF16 = jnp.bfloat16

RMS_EPS = 1e-6
ROPE_THETA = 10000.0
HEAD_DIM = 64
ROPE_HALF = HEAD_DIM // 2

V7X_VMEM_BYTES = 64 * 1024 * 1024
VMEM_LIMIT = (V7X_VMEM_BYTES * 3) // 4
LANES = 128

TOKEN_TILE = 512
FF_CHUNK = 512


def _cparams(*sem):
    return pltpu.CompilerParams(dimension_semantics=sem, vmem_limit_bytes=VMEM_LIMIT)


def _rms(x, g):
    ms = jnp.mean(x * x, axis=-1, keepdims=True)
    return x * lax.rsqrt(ms + RMS_EPS) * g


def _dot(a, b):
    return jnp.dot(a, b, preferred_element_type=F32)


def _dot_nt(a, b):
    return lax.dot_general(a, b, (((1,), (1,)), ((), ())), preferred_element_type=F32)


def _const_spec(shape):
    nd = len(shape)
    return pl.BlockSpec(shape, lambda *_: (0,) * nd)


def _ffn_kernel(h_ref, gpre_ref, gpost_ref, wgu_ref, wd_ref, o_ref):
    d_ff = wd_ref.shape[0]
    h = h_ref[...]
    xn = _rms(h, gpre_ref[...]).astype(BF16)
    acc = jnp.zeros(h.shape, F32)
    for c in range(d_ff // FF_CHUNK):
        lo = c * FF_CHUNK
        gate = _dot(xn, wgu_ref[:, lo:lo + FF_CHUNK])
        up = _dot(xn, wgu_ref[:, d_ff + lo:d_ff + lo + FF_CHUNK])
        act = (gate * jax.nn.sigmoid(gate) * up).astype(BF16)
        acc = acc + _dot(act, wd_ref[lo:lo + FF_CHUNK, :])
    o_ref[...] = h + 0.5 * _rms(acc, gpost_ref[...])


def _ffn(h, g_pre, g_post, w_gu, w_down):
    t, d = h.shape
    d_ff = w_down.shape[0]
    tm = min(TOKEN_TILE, t)
    return pl.pallas_call(
        _ffn_kernel,
        grid=(t // tm,),
        in_specs=[
            pl.BlockSpec((tm, d), lambda i: (i, 0)),
            _const_spec((1, d)),
            _const_spec((1, d)),
            _const_spec((d, 2 * d_ff)),
            _const_spec((d_ff, d)),
        ],
        out_specs=pl.BlockSpec((tm, d), lambda i: (i, 0)),
        out_shape=jax.ShapeDtypeStruct((t, d), F32),
        compiler_params=_cparams("parallel"),
        name="ffn",
    )(h, g_pre, g_post, w_gu, w_down)


def _memkv_kernel(mem_ref, g_ref, wkv_ref, k_ref, v_ref):
    d = mem_ref.shape[-1]
    mn = _rms(mem_ref[...], g_ref[...]).astype(BF16)
    kv = _dot(mn, wkv_ref[...])
    k_ref[...] = kv[:, :d].astype(BF16)
    v_ref[...] = kv[:, d:].astype(BF16)


def _memkv(mem, g_mem, wkv):
    b, m, d = mem.shape
    return pl.pallas_call(
        _memkv_kernel,
        grid=(b,),
        in_specs=[
            pl.BlockSpec((None, m, d), lambda i: (i, 0, 0)),
            _const_spec((1, d)),
            _const_spec((d, 2 * d)),
        ],
        out_specs=[pl.BlockSpec((None, m, d), lambda i: (i, 0, 0))] * 2,
        out_shape=[jax.ShapeDtypeStruct((b, m, d), BF16)] * 2,
        compiler_params=_cparams("parallel"),
        name="memkv",
    )(mem, g_mem, wkv)


def _cross_kernel(n_heads, h_ref, gpre_ref, gpost_ref, wq_ref, k_ref, v_ref, wo_ref, o_ref):
    h = h_ref[...]
    d = h.shape[-1]
    hd = d // n_heads
    hn = _rms(h, gpre_ref[...]).astype(BF16)
    q = (_dot(hn, wq_ref[...]) * (hd ** -0.5)).astype(BF16)
    c = jnp.zeros(h.shape, F32)
    for a in range(n_heads):
        sl = slice(a * hd, (a + 1) * hd)
        s = _dot_nt(q[:, sl], k_ref[:, sl])
        e = jnp.exp(s - jnp.max(s, axis=-1, keepdims=True))
        p = e / jnp.sum(e, axis=-1, keepdims=True)
        o = _dot(p.astype(BF16), v_ref[:, sl]).astype(BF16)
        c = c + _dot(o, wo_ref[sl, :])
    o_ref[...] = h + _rms(c, gpost_ref[...])


def _cross(h, g_pre, g_post, wq, k, v, wo, n_heads):
    b, s, d = h.shape
    m = k.shape[1]
    tm = min(TOKEN_TILE, s)
    return pl.pallas_call(
        functools.partial(_cross_kernel, n_heads),
        grid=(b, s // tm),
        in_specs=[
            pl.BlockSpec((None, tm, d), lambda i, j: (i, j, 0)),
            _const_spec((1, d)),
            _const_spec((1, d)),
            _const_spec((d, d)),
            pl.BlockSpec((None, m, d), lambda i, j: (i, 0, 0)),
            pl.BlockSpec((None, m, d), lambda i, j: (i, 0, 0)),
            _const_spec((d, d)),
        ],
        out_specs=pl.BlockSpec((None, tm, d), lambda i, j: (i, j, 0)),
        out_shape=jax.ShapeDtypeStruct((b, s, d), F32),
        compiler_params=_cparams("parallel", "parallel"),
        name="cross",
    )(h, g_pre, g_post, wq, k, v, wo)


def _outproj_kernel(n_in, *refs):
    a_refs = refs[:n_in]
    w_refs = refs[n_in:2 * n_in]
    h_ref, g_ref, o_ref = refs[2 * n_in:]
    m = _dot(a_refs[0][...], w_refs[0][...])
    for a_ref, w_ref in zip(a_refs[1:], w_refs[1:]):
        m = m + _dot(a_ref[...], w_ref[...])
    o_ref[...] = h_ref[...] + _rms(m, g_ref[...])


def _outproj(acts, weights, h, g):
    t, d = h.shape
    tm = min(TOKEN_TILE, t)
    n_in = len(acts)
    return pl.pallas_call(
        functools.partial(_outproj_kernel, n_in),
        grid=(t // tm,),
        in_specs=(
            [pl.BlockSpec((tm, a.shape[1]), lambda i: (i, 0)) for a in acts]
            + [_const_spec(w.shape) for w in weights]
            + [pl.BlockSpec((tm, d), lambda i: (i, 0)), _const_spec((1, d))]
        ),
        out_specs=pl.BlockSpec((tm, d), lambda i: (i, 0)),
        out_shape=jax.ShapeDtypeStruct((t, d), F32),
        compiler_params=_cparams("parallel"),
        name="outproj",
    )(*acts, *weights, h, g)


SSD_HEADDIM = 64
SSD_GROUPS = 4
SSD_STATE = 128
SSD_CONV = 4
SSD_L = 128
CONV_HALO = 8
NEG_BIG = -1e30


def _split3(x):
    a = x.astype(BF16)
    r = x - a.astype(F32)
    b = r.astype(BF16)
    c = (r - b.astype(F32)).astype(BF16)
    return a, b, c


def _softplus(x):
    return jnp.maximum(x, 0.0) + jnp.log(1.0 + jnp.exp(-jnp.abs(x)))


def _ssd_proj_kernel(h_ref, g_ref, w_ref, z_ref, xbc_ref, dt_ref):
    xn = _rms(h_ref[...], g_ref[...]).astype(BF16)
    inner = z_ref.shape[-1]
    cc = xbc_ref.shape[-1]
    for lo in range(0, inner, FF_CHUNK):
        z_ref[:, lo:lo + FF_CHUNK] = _dot(xn, w_ref[:, lo:lo + FF_CHUNK]).astype(BF16)
    for lo in range(0, cc, FF_CHUNK):
        xbc_ref[:, lo:lo + FF_CHUNK] = _dot(xn, w_ref[:, inner + lo:inner + lo + FF_CHUNK]).astype(BF16)
    dt_ref[...] = _dot(xn, w_ref[:, inner + cc:])


def _ssd_proj(h, g, w, inner, cc):
    t, d = h.shape
    tm = min(TOKEN_TILE, t)
    row = lambda n: pl.BlockSpec((tm, n), lambda i: (i, 0))
    return pl.pallas_call(
        _ssd_proj_kernel,
        grid=(t // tm,),
        in_specs=[row(d), _const_spec((1, d)), _const_spec(w.shape)],
        out_specs=[row(inner), row(cc), row(LANES)],
        out_shape=[jax.ShapeDtypeStruct((t, inner), BF16), jax.ShapeDtypeStruct((t, cc), BF16),
                   jax.ShapeDtypeStruct((t, LANES), F32)],
        compiler_params=_cparams("parallel"),
        name="ssd_proj",
    )(h, g, w)


def _ssd_scan_kernel(xbc_ref, dt_ref, z_ref, cw_ref, cb_ref, dtb_ref, alog_ref, dskip_ref, ng_ref,
                     y_ref, buf_ref, xc_ref, state_ref, ys_ref):
    L = xbc_ref.shape[0]
    cc = xbc_ref.shape[1]
    inner = z_ref.shape[1]
    G, N, P = SSD_GROUPS, SSD_STATE, SSD_HEADDIM
    hpg = inner // P // G
    gw = hpg * P
    H0 = CONV_HALO

    @pl.when(pl.program_id(1) == 0)
    def _():
        buf_ref[0:H0, :] = jnp.zeros((H0, cc), F32)
        state_ref[...] = jnp.zeros(state_ref.shape, F32)

    buf_ref[H0:H0 + L, :] = xbc_ref[...].astype(F32)
    for lo in range(0, cc, FF_CHUNK):
        sl = slice(lo, lo + FF_CHUNK)
        acc = cb_ref[:, sl] + cw_ref[0:1, sl] * buf_ref[H0 - 3:H0 - 3 + L, sl]
        for k in range(1, SSD_CONV):
            acc = acc + cw_ref[k:k + 1, sl] * buf_ref[H0 - 3 + k:H0 - 3 + k + L, sl]
        xc_ref[:, sl] = acc * jax.nn.sigmoid(acc)
    buf_ref[0:H0, :] = buf_ref[L:L + H0, :]

    dt = _softplus(dt_ref[...] + dtb_ref[...])
    da = dt * (-jnp.exp(alog_ref[...]))
    ti = lax.broadcasted_iota(jnp.int32, (L, L), 0)
    si = lax.broadcasted_iota(jnp.int32, (L, L), 1)
    causal = si <= ti
    tri = jnp.where(causal, 1.0, 0.0).astype(BF16)
    d1, d2, d3 = _split3(da)
    acum = _dot(tri, d1) + _dot(tri, d2) + _dot(tri, d3)
    acum_t = acum.T
    dt_t = dt.T
    e_acum = jnp.exp(acum)
    last = acum[L - 1:L, :]
    dec_end = jnp.exp(last - acum) * dt
    e_last = jnp.exp(last)
    lane = lax.broadcasted_iota(jnp.int32, (L, 2 * P), 1)
    first = lane < P
    lane1 = lax.broadcasted_iota(jnp.int32, (1, 2 * P), 1)
    first1 = lane1 < P

    for g in range(G):
        bg = xc_ref[:, inner + g * N:inner + (g + 1) * N]
        cg = xc_ref[:, inner + G * N + g * N:inner + G * N + (g + 1) * N]
        cbm = _dot_nt(cg.astype(BF16), bg.astype(BF16))
        dx_parts = []
        sd_parts = []
        for pr in range(hpg // 2):
            h0 = g * hpg + 2 * pr
            xsl = slice(h0 * P, (h0 + 2) * P)
            x_pair = xc_ref[:, xsl]
            st_pair = state_ref[g, :, 2 * pr * P:(2 * pr + 2) * P]
            rhs = jnp.concatenate([x_pair, st_pair], axis=0).astype(BF16)
            ys = []
            for h in (h0, h0 + 1):
                col = acum[:, h:h + 1]
                seg = col - acum_t[h:h + 1, :]
                decay = jnp.exp(jnp.where(causal, seg, NEG_BIG))
                w = cbm * decay * dt_t[h:h + 1, :]
                cs = cg * e_acum[:, h:h + 1]
                lhs = jnp.concatenate([w, cs], axis=1).astype(BF16)
                ys.append(_dot(lhs, rhs))
            y_pair = jnp.where(first, ys[0], ys[1]) + dskip_ref[:, xsl] * x_pair
            zp = z_ref[:, xsl].astype(F32)
            ys_ref[:, xsl] = y_pair * (zp * jax.nn.sigmoid(zp))
            f = jnp.where(first, dec_end[:, h0:h0 + 1], dec_end[:, h0 + 1:h0 + 2])
            dx_parts.append((x_pair * f).astype(BF16))
            sd_parts.append(jnp.where(first1, e_last[:, h0:h0 + 1], e_last[:, h0 + 1:h0 + 2]))
        dx = jnp.concatenate(dx_parts, axis=1)
        sd = jnp.concatenate(sd_parts, axis=1)
        upd = lax.dot_general(bg.astype(BF16), dx, (((0,), (0,)), ((), ())), preferred_element_type=F32)
        state_ref[g] = state_ref[g] * sd + upd

    y_ref[...] = _rms(ys_ref[...], ng_ref[...]).astype(BF16)


def _ssd_scan(xbc, dt, z, conv_w, conv_b, dt_bias, a_log, d_skip, norm_g):
    b, s, cc = xbc.shape
    inner = z.shape[-1]
    L = min(SSD_L, s)
    G, N = SSD_GROUPS, SSD_STATE
    blk = lambda n: pl.BlockSpec((None, L, n), lambda i, j: (i, j, 0))
    return pl.pallas_call(
        _ssd_scan_kernel,
        grid=(b, s // L),
        in_specs=[blk(cc), blk(LANES), blk(inner), _const_spec(conv_w.shape), _const_spec((1, cc)),
                  _const_spec((1, LANES)), _const_spec((1, LANES)), _const_spec((1, inner)),
                  _const_spec((1, inner))],
        out_specs=blk(inner),
        out_shape=jax.ShapeDtypeStruct((b, s, inner), BF16),
        scratch_shapes=[pltpu.VMEM((CONV_HALO + L, cc), F32), pltpu.VMEM((L, cc), F32),
                        pltpu.VMEM((G, N, inner // G), F32), pltpu.VMEM((L, inner), F32)],
        compiler_params=_cparams("parallel", "arbitrary"),
        name="ssd_scan",
    )(xbc, dt, z, conv_w, conv_b, dt_bias, a_log, d_skip, norm_g)


def _pad_lanes(v):
    return jnp.zeros((1, LANES), F32).at[0, :v.shape[0]].set(v.astype(F32))


def _prep_ssd(w_in, conv_w, conv_b, dt_bias, a_log, d_skip, norm_g, w_out):
    d, n_in = w_in.shape
    n_heads = dt_bias.shape[0]
    inner = n_heads * SSD_HEADDIM
    cc = conv_w.shape[1]
    w = jnp.zeros((d, inner + cc + LANES), BF16).at[:, :n_in].set(w_in.astype(BF16))
    return (w, conv_w.astype(F32), conv_b.astype(F32)[None], _pad_lanes(dt_bias), _pad_lanes(a_log),
            jnp.repeat(d_skip.astype(F32), SSD_HEADDIM)[None], norm_g.astype(F32)[None], w_out.astype(BF16))


def _ssd_layer(h, g_pre, g_post, w, conv_w, conv_b, dt_bias, a_log, d_skip, norm_g, w_out):
    b, s, d = h.shape
    inner = norm_g.shape[-1]
    cc = conv_w.shape[1]
    hf = h.reshape(b * s, d)
    z, xbc, dt = _ssd_proj(hf, g_pre, w, inner, cc)
    y = _ssd_scan(xbc.reshape(b, s, cc), dt.reshape(b, s, LANES), z.reshape(b, s, inner),
                  conv_w, conv_b, dt_bias, a_log, d_skip, norm_g)
    return _outproj([y.reshape(b * s, inner)], [w_out], hf, g_post).reshape(b, s, d)


SB_HEADS = 8
NSA_HEADS = 8
NSA_GROUPS = 2
NSA_REP = NSA_HEADS // NSA_GROUPS
CMP_LEN = 32
CMP_STRIDE = 16
SEL_BLOCK = 64
SEL_TOPN = 16
WINDOW = 512
FORCE_BONUS = 1e4
NSA_QB = 128
SB_W = SB_HEADS * HEAD_DIM
NSA_QW = NSA_HEADS * HEAD_DIM
NSA_KVW = NSA_GROUPS * HEAD_DIM
ATT_SCALE = HEAD_DIM ** -0.5

_C_QSB, _C_KSB, _C_VSB = 0, SB_W, 2 * SB_W
_C_QN = 3 * SB_W
_C_QNP = _C_QN + NSA_QW
_C_K3 = _C_QNP + NSA_QW
_C_K3P = _C_K3 + 3 * NSA_KVW
_C_V3 = _C_K3P + 3 * NSA_KVW
_C_GATE = _C_V3 + 3 * NSA_KVW
_C_END = _C_GATE + LANES


def _hyb_proj_kernel(h_ref, g_ref, pos_ref, inv_ref, sgn_ref, w_ref,
                     qsb_ref, ksb_ref, vsb_ref, qn_ref, kcmp_ref, vcmp_ref,
                     ksel_ref, vsel_ref, kwin_ref, vwin_ref, gate_ref):
    xn = _rms(h_ref[...], g_ref[...]).astype(BF16)
    tm = xn.shape[0]
    hd = HEAD_DIM

    def proj(lo, n):
        return _dot(xn, w_ref[:, lo:lo + n])

    for ref, lo, scale in ((qsb_ref, _C_QSB, ATT_SCALE), (ksb_ref, _C_KSB, 1.0), (vsb_ref, _C_VSB, 1.0)):
        p = proj(lo, SB_W) * scale
        for a in range(SB_HEADS):
            ref[a] = p[:, a * hd:(a + 1) * hd].astype(BF16)

    ang = pos_ref[...] * inv_ref[...]
    cos = jnp.cos(ang)
    sin = jnp.sin(ang) * sgn_ref[...]

    def rot(lo, lop, n):
        reps = n // LANES
        return (proj(lo, n) * jnp.concatenate([cos] * reps, axis=1)
                + proj(lop, n) * jnp.concatenate([sin] * reps, axis=1))

    qn = rot(_C_QN, _C_QNP, NSA_QW) * ATT_SCALE
    lane = lax.broadcasted_iota(jnp.int32, (tm, LANES), 1)
    low = lane < hd
    for a in range(NSA_HEADS):
        grp = a // NSA_REP
        piece = qn[:, (a // 2) * LANES:(a // 2 + 1) * LANES]
        if (a % 2) != grp:
            piece = pltpu.roll(piece, hd, 1)
        keep = low if grp == 0 else jnp.logical_not(low)
        qn_ref[a] = jnp.where(keep, piece, 0.0).astype(BF16)

    k3 = rot(_C_K3, _C_K3P, 3 * NSA_KVW)
    v3 = proj(_C_V3, 3 * NSA_KVW)
    for grp in range(NSA_GROUPS):
        kcmp_ref[grp] = k3[:, grp * hd:(grp + 1) * hd].astype(BF16)
        vcmp_ref[grp] = v3[:, grp * hd:(grp + 1) * hd].astype(BF16)
    ksel_ref[...] = k3[:, NSA_KVW:2 * NSA_KVW].astype(BF16)
    vsel_ref[...] = v3[:, NSA_KVW:2 * NSA_KVW].astype(BF16)
    kwin_ref[...] = k3[:, 2 * NSA_KVW:].astype(BF16)
    vwin_ref[...] = v3[:, 2 * NSA_KVW:].astype(BF16)
    gate_ref[...] = jax.nn.sigmoid(proj(_C_GATE, LANES))


def _hyb_proj(h, g, posf, inv, sgn, w):
    b, s, d = h.shape
    tm = min(TOKEN_TILE, s)
    heads = lambda n, w_: pl.BlockSpec((None, n, tm, w_), lambda i, j: (i, 0, j, 0))
    rows = lambda w_: pl.BlockSpec((None, tm, w_), lambda i, j: (i, j, 0))
    hshape = lambda n, w_: jax.ShapeDtypeStruct((b, n, s, w_), BF16)
    rshape = lambda w_, dt: jax.ShapeDtypeStruct((b, s, w_), dt)
    return pl.pallas_call(
        _hyb_proj_kernel,
        grid=(b, s // tm),
        in_specs=[rows(d), _const_spec((1, d)), rows(1), _const_spec((1, LANES)), _const_spec((1, LANES)),
                  _const_spec(w.shape)],
        out_specs=[heads(SB_HEADS, HEAD_DIM)] * 3 + [heads(NSA_HEADS, LANES)]
        + [heads(NSA_GROUPS, HEAD_DIM)] * 2 + [rows(LANES)] * 5,
        out_shape=[hshape(SB_HEADS, HEAD_DIM)] * 3 + [hshape(NSA_HEADS, LANES)]
        + [hshape(NSA_GROUPS, HEAD_DIM)] * 2 + [rshape(LANES, BF16)] * 4 + [rshape(LANES, F32)],
        compiler_params=_cparams("parallel", "parallel"),
        name="hyb_proj",
    )(h, g, posf, inv, sgn, w)


SB_TILE = 512
SB_SUB = 256


def _sb_kernel(qi_ref, kj_ref, q_ref, k_ref, v_ref, o_ref, acc_ref, carry_ref):
    p = pl.program_id(1)
    i = qi_ref[p]
    j = kj_ref[p]
    n_heads, tq, hd = q_ref.shape
    tk = k_ref.shape[1]
    sub = min(SB_SUB, tk)

    @pl.when(j == i)
    def _():
        acc_ref[...] = jnp.zeros(acc_ref.shape, F32)
        carry_ref[...] = jnp.zeros(carry_ref.shape, F32)

    kr = lax.broadcasted_iota(jnp.int32, (sub, sub), 0)
    kc = lax.broadcasted_iota(jnp.int32, (sub, sub), 1)
    later = jnp.where(kr > kc, 1.0, 0.0).astype(BF16)

    def tile(diag):
        for a in range(n_heads):
            for sb in reversed(range(tk // sub)):
                r0 = sb * sub if diag else 0
                rows = tq - r0
                q = q_ref[a, r0:, :]
                k = k_ref[a, sb * sub:(sb + 1) * sub, :]
                v = v_ref[a, sb * sub:(sb + 1) * sub, :]
                z = _dot_nt(q, k)
                ls = -(jnp.maximum(z, 0.0) + jnp.log(1.0 + jnp.exp(-jnp.abs(z))))
                if diag:
                    qpos = lax.broadcasted_iota(jnp.int32, (rows, sub), 0)
                    kpos = lax.broadcasted_iota(jnp.int32, (rows, sub), 1)
                    mask = kpos < qpos
                    ls = jnp.where(mask, ls, 0.0)
                hi = ls.astype(BF16)
                lo = (ls - hi.astype(F32)).astype(BF16)
                cs = _dot(hi, later) + _dot(lo, later)
                carry = carry_ref[a, r0:, :]
                w = jnp.exp(z + ls + cs + carry)
                if diag:
                    w = jnp.where(mask, w, 0.0)
                acc_ref[a, r0:, :] += _dot(w.astype(BF16), v)
                carry_ref[a, r0:, :] = carry + cs[:, 0:1] + ls[:, 0:1]

    pl.when(j == i)(lambda: tile(True))
    pl.when(j < i)(lambda: tile(False))

    @pl.when(j == 0)
    def _():
        for a in range(n_heads):
            o_ref[:, a * hd:(a + 1) * hd] = acc_ref[a].astype(o_ref.dtype)


def _sb_attn(q, k, v):
    b, n_heads, s, hd = q.shape
    t = min(SB_TILE, s)
    nq = s // t
    pairs = [(i, j) for i in range(nq) for j in range(i, -1, -1)]
    qi = jnp.asarray([p[0] for p in pairs], jnp.int32)
    kj = jnp.asarray([p[1] for p in pairs], jnp.int32)
    grid_spec = pltpu.PrefetchScalarGridSpec(
        num_scalar_prefetch=2,
        grid=(b, len(pairs)),
        in_specs=[
            pl.BlockSpec((None, n_heads, t, hd), lambda bi, p, qi, kj: (bi, 0, qi[p], 0)),
            pl.BlockSpec((None, n_heads, t, hd), lambda bi, p, qi, kj: (bi, 0, kj[p], 0)),
            pl.BlockSpec((None, n_heads, t, hd), lambda bi, p, qi, kj: (bi, 0, kj[p], 0)),
        ],
        out_specs=pl.BlockSpec((None, t, n_heads * hd), lambda bi, p, qi, kj: (bi, qi[p], 0)),
        scratch_shapes=[pltpu.VMEM((n_heads, t, hd), F32), pltpu.VMEM((n_heads, t, 1), F32)],
    )
    return pl.pallas_call(
        _sb_kernel,
        grid_spec=grid_spec,
        out_shape=jax.ShapeDtypeStruct((b, s, n_heads * hd), BF16),
        compiler_params=_cparams("parallel", "arbitrary"),
        name="sb_attn",
    )(qi, kj, q, k, v)


def _compress_kernel(t_ref, wab_ref, pe_ref, w1_ref, w2_ref, o_ref):
    n_grp, nch, _ = t_ref.shape
    hid = w2_ref.shape[0]
    pe8 = jnp.broadcast_to(pe_ref[...], (8, pe_ref.shape[1])).astype(BF16)
    bias = _dot(pe8, w1_ref[...])[0:1, :]
    for grp in range(n_grp):
        ab = _dot(t_ref[grp], wab_ref[...])
        nxt = pltpu.roll(ab[:, hid:], nch - 1, 0)
        hcur = ab[:, :hid] + nxt + bias
        act = (hcur * jax.nn.sigmoid(hcur)).astype(BF16)
        o_ref[:, grp * HEAD_DIM:(grp + 1) * HEAD_DIM] = _dot(act, w2_ref[...]).astype(o_ref.dtype)


def _compress(t, wab, pe, w1, w2):
    b, n_grp, nch, kw = t.shape
    return pl.pallas_call(
        _compress_kernel,
        grid=(b,),
        in_specs=[pl.BlockSpec((None, n_grp, nch, kw), lambda i: (i, 0, 0, 0)), _const_spec(wab.shape),
                  _const_spec(pe.shape), _const_spec(w1.shape), _const_spec(w2.shape)],
        out_specs=pl.BlockSpec((None, nch, n_grp * HEAD_DIM), lambda i: (i, 0, 0)),
        out_shape=jax.ShapeDtypeStruct((b, nch, n_grp * HEAD_DIM), BF16),
        compiler_params=_cparams("parallel"),
        name="nsa_compress",
    )(t, wab, pe, w1, w2)


def _masked_softmax_rows(s, mask):
    sm = jnp.where(mask, s, NEG_BIG)
    m = jnp.max(sm, axis=-1, keepdims=True)
    e = jnp.where(mask, jnp.exp(sm - m), 0.0)
    den = jnp.sum(e, axis=-1, keepdims=True)
    return e / jnp.maximum(den, 1e-30)


def _select_kernel(qf_ref, kc_ref, o_ref):
    n_heads, nsub, _ = qf_ref.shape
    nch = kc_ref.shape[0]
    q = qf_ref[...].reshape(n_heads * nsub, LANES)
    s = _dot_nt(q, kc_ref[...])
    sub_id = lax.broadcasted_iota(jnp.int32, (n_heads * nsub, nch), 0) % nsub
    c_id = lax.broadcasted_iota(jnp.int32, (n_heads * nsub, nch), 1)
    p = _masked_softmax_rows(s, c_id * CMP_STRIDE + (CMP_LEN - 1) <= sub_id * SEL_BLOCK)
    rows = NSA_GROUPS * nsub
    psum = jnp.concatenate(
        [sum(p[(grp * NSA_REP + r) * nsub:(grp * NSA_REP + r + 1) * nsub] for r in range(NSA_REP))
         for grp in range(NSA_GROUPS)], axis=0)
    oc = lax.broadcasted_iota(jnp.int32, (nch, LANES), 0) * CMP_STRIDE
    oj = lax.broadcasted_iota(jnp.int32, (nch, LANES), 1) * SEL_BLOCK
    overlap = jnp.where((oc < oj + SEL_BLOCK) & (oc + (CMP_LEN - 1) >= oj), 1.0, 0.0).astype(BF16)
    p1, p2, p3 = _split3(psum)
    imp = _dot(p1, overlap) + _dot(p2, overlap) + _dot(p3, overlap)
    jl = lax.broadcasted_iota(jnp.int32, (rows, LANES), 1)
    cur = lax.broadcasted_iota(jnp.int32, (rows, LANES), 0) % nsub
    forced = (jl == 0) | (jl == cur) | (jl == cur - 1)
    score = jnp.where(jl <= cur, imp + jnp.where(forced, FORCE_BONUS, 0.0), -1.0)

    def body(d, rank):
        other = pltpu.roll(score, d, 1)
        ahead = (other > score) | ((other == score) & (jl >= d))
        return rank + jnp.where(ahead, 1.0, 0.0)

    rank = lax.fori_loop(1, LANES, body, jnp.zeros((rows, LANES), F32))
    jf = jl.astype(F32)
    out = jnp.zeros((rows, LANES), F32)
    for r in range(SEL_TOPN):
        col = jnp.sum(jnp.where(rank == float(r), jf, 0.0), axis=-1, keepdims=True)
        out = jnp.where(jl == r, col, out)
    o_ref[...] = out.astype(jnp.int32)


def _nsa_select(qf, kc):
    b, n_heads, nsub, _ = qf.shape
    nch = kc.shape[1]
    assert SEL_TOPN <= nsub <= LANES
    return pl.pallas_call(
        _select_kernel,
        grid=(b,),
        in_specs=[pl.BlockSpec((None, n_heads, nsub, LANES), lambda i: (i, 0, 0, 0)),
                  pl.BlockSpec((None, nch, LANES), lambda i: (i, 0, 0))],
        out_specs=pl.BlockSpec((None, NSA_GROUPS * nsub, LANES), lambda i: (i, 0, 0)),
        out_shape=jax.ShapeDtypeStruct((b, NSA_GROUPS * nsub, LANES), jnp.int32),
        compiler_params=_cparams("parallel"),
        name="nsa_select",
    )(qf, kc)


def _nsa_kernel(sel_ref, qn_ref, kc_ref, vc_ref, ksel_ref, vsel_ref, kwin_ref, vwin_ref, gate_ref,
                o_ref, kg_ref, vg_ref):
    bi = pl.program_id(0)
    i = pl.program_id(1)
    n_heads, qb, _ = qn_ref.shape
    s_len = ksel_ref.shape[0]
    nsub_total = s_len // SEL_BLOCK
    nch = kc_ref.shape[0]
    hd = HEAD_DIM
    q = qn_ref[...].reshape(n_heads * qb, LANES)
    qpos = i * qb + lax.broadcasted_iota(jnp.int32, (n_heads * qb, 1), 0) % qb

    c_end = lax.broadcasted_iota(jnp.int32, (1, nch), 1) * CMP_STRIDE + (CMP_LEN - 1)
    p_c = _masked_softmax_rows(_dot_nt(q, kc_ref[...]), c_end <= qpos)
    o_c = _dot(p_c.astype(BF16), vc_ref[...])

    wlen = WINDOW + qb
    start = pl.multiple_of(jnp.maximum(i * qb - WINDOW, 0), qb)
    kpos = start + lax.broadcasted_iota(jnp.int32, (1, wlen), 1)
    delta = qpos - kpos
    p_w = _masked_softmax_rows(_dot_nt(q, kwin_ref[pl.ds(start, wlen), :]), (delta >= 0) & (delta < WINDOW))
    o_w = _dot(p_w.astype(BF16), vwin_ref[pl.ds(start, wlen), :])

    nsel = SEL_TOPN * SEL_BLOCK
    lane_blk = lax.broadcasted_iota(jnp.int32, (1, nsel), 1) // SEL_BLOCK
    lane_off = lax.broadcasted_iota(jnp.int32, (1, nsel), 1) % SEL_BLOCK
    n_sub = qb // SEL_BLOCK
    o_s = [[None] * n_sub for _ in range(NSA_GROUPS)]
    for grp in range(NSA_GROUPS):
        for n in range(n_sub):
            base = ((bi * NSA_GROUPS + grp) * nsub_total + i * n_sub + n) * SEL_TOPN
            tok = lane_off
            for t in range(SEL_TOPN):
                idx = sel_ref[base + t]
                off = pl.multiple_of(idx * SEL_BLOCK, SEL_BLOCK)
                kg_ref[t * SEL_BLOCK:(t + 1) * SEL_BLOCK, :] = ksel_ref[pl.ds(off, SEL_BLOCK), :]
                vg_ref[t * SEL_BLOCK:(t + 1) * SEL_BLOCK, :] = vsel_ref[pl.ds(off, SEL_BLOCK), :]
                tok = tok + jnp.where(lane_blk == t, idx * SEL_BLOCK, 0)
            qs = qn_ref[grp * NSA_REP:(grp + 1) * NSA_REP, n * SEL_BLOCK:(n + 1) * SEL_BLOCK, :]
            qs = qs.reshape(NSA_REP * SEL_BLOCK, LANES)
            qp = (i * qb + n * SEL_BLOCK
                  + lax.broadcasted_iota(jnp.int32, (NSA_REP * SEL_BLOCK, 1), 0) % SEL_BLOCK)
            p_s = _masked_softmax_rows(_dot_nt(qs, kg_ref[...]), tok <= qp)
            o_s[grp][n] = _dot(p_s.astype(BF16), vg_ref[...])

    lane = lax.broadcasted_iota(jnp.int32, (qb, LANES), 1)
    low = lane < hd
    gates = gate_ref[...]
    mixed = []
    for a in range(n_heads):
        grp, r = a // NSA_REP, a % NSA_REP
        rs = slice(a * qb, (a + 1) * qb)
        sel_rows = jnp.concatenate([o_s[grp][n][r * SEL_BLOCK:(r + 1) * SEL_BLOCK] for n in range(n_sub)],
                                   axis=0)
        mixed.append(gates[:, 3 * a:3 * a + 1] * o_c[rs] + gates[:, 3 * a + 1:3 * a + 2] * sel_rows
                     + gates[:, 3 * a + 2:3 * a + 3] * o_w[rs])
    for pr in range(n_heads // 2):
        grp = (2 * pr) // NSA_REP
        left, right = mixed[2 * pr], mixed[2 * pr + 1]
        if grp == 0:
            right = pltpu.roll(right, hd, 1)
        else:
            left = pltpu.roll(left, hd, 1)
        o_ref[:, pr * LANES:(pr + 1) * LANES] = jnp.where(low, left, right).astype(o_ref.dtype)


def _nsa_attn(sel, qn, kc, vc, ksel, vsel, kwin, vwin, gates):
    b, n_heads, s, _ = qn.shape
    nch = kc.shape[1]
    qb = min(NSA_QB, s)
    assert s >= WINDOW + qb
    full = lambda n: pl.BlockSpec((None, n, LANES), lambda bi, i, sel: (bi, 0, 0))
    grid_spec = pltpu.PrefetchScalarGridSpec(
        num_scalar_prefetch=1,
        grid=(b, s // qb),
        in_specs=[pl.BlockSpec((None, n_heads, qb, LANES), lambda bi, i, sel: (bi, 0, i, 0)),
                  full(nch), full(nch), full(s), full(s), full(s), full(s),
                  pl.BlockSpec((None, qb, LANES), lambda bi, i, sel: (bi, i, 0))],
        out_specs=pl.BlockSpec((None, qb, n_heads * HEAD_DIM), lambda bi, i, sel: (bi, i, 0)),
        scratch_shapes=[pltpu.VMEM((SEL_TOPN * SEL_BLOCK, LANES), BF16)] * 2,
    )
    return pl.pallas_call(
        _nsa_kernel,
        grid_spec=grid_spec,
        out_shape=jax.ShapeDtypeStruct((b, s, n_heads * HEAD_DIM), BF16),
        compiler_params=_cparams("parallel", "arbitrary"),
        name="nsa_attn",
    )(sel, qn, kc, vc, ksel, vsel, kwin, vwin, gates)


def _prep_hybrid(w_in, w_out, pe_k, w1_k, w2_k, pe_v, w1_v, w2_v):
    d = w_in.shape[0]
    hd = HEAD_DIM
    col = np.arange(w_in.shape[1])
    o_qn = 3 * SB_W
    o_kv = o_qn + NSA_QW
    kv = lambda n: col[o_kv + n * NSA_KVW:o_kv + (n + 1) * NSA_KVW]
    partner = lambda c: (c // hd) * hd + (c % hd + hd // 2) % hd
    qn_c = col[o_qn:o_qn + NSA_QW]
    k3_c = np.concatenate([kv(0), kv(2), kv(4)])
    v3_c = np.concatenate([kv(1), kv(3), kv(5)])
    gate_c = col[o_kv + 6 * NSA_KVW:]
    order = np.concatenate([col[:o_qn], qn_c, partner(qn_c), k3_c, partner(k3_c), v3_c, gate_c])
    w = jnp.zeros((d, _C_END), BF16).at[:, :order.shape[0]].set(w_in[:, order].astype(BF16))
    lanes = np.arange(LANES)
    inv = (ROPE_THETA ** (-(lanes % ROPE_HALF).astype(np.float32) / ROPE_HALF)).astype(np.float32)[None]
    sgn = np.where(lanes % hd < ROPE_HALF, -1.0, 1.0).astype(np.float32)[None]

    def cmp_w(pe, w1, w2):
        half = w1.shape[0] // 2
        wab = jnp.concatenate([w1[:half], w1[half:]], axis=1).astype(BF16)
        return wab, pe.reshape(1, -1).astype(F32), w1.astype(BF16), w2.astype(BF16)

    return (w, jnp.asarray(inv), jnp.asarray(sgn), w_out[:SB_W].astype(BF16), w_out[SB_W:].astype(BF16),
            cmp_w(pe_k, w1_k, w2_k), cmp_w(pe_v, w1_v, w2_v))


def _hybrid_layer(h, positions, g_pre, g_post, w, inv, sgn, wo_sb, wo_nsa, cmp_k, cmp_v):
    b, s, d = h.shape
    posf = positions.astype(F32)[..., None]
    (qsb, ksb, vsb, qn, kcmp, vcmp, ksel, vsel, kwin, vwin, gates) = _hyb_proj(h, g_pre, posf, inv, sgn, w)
    o_sb = _sb_attn(qsb, ksb, vsb)
    half = CMP_LEN // 2
    chunks = lambda t: t.reshape(b, NSA_GROUPS, s // half, half * HEAD_DIM)
    kc = _compress(chunks(kcmp), *cmp_k)
    vc = _compress(chunks(vcmp), *cmp_v)
    sel = _nsa_select(qn[:, :, ::SEL_BLOCK, :], kc)
    sel = sel[:, :, :SEL_TOPN].reshape(-1)
    o_nsa = _nsa_attn(sel, qn, kc, vc, ksel, vsel, kwin, vwin, gates)
    hf = h.reshape(b * s, d)
    return _outproj([o_sb.reshape(b * s, SB_W), o_nsa.reshape(b * s, NSA_QW)], [wo_sb, wo_nsa], hf,
                    g_post).reshape(b, s, d)


CROSS_HEADS = 4


def kernel(x, mem, positions, norm_g, ffn1_w_gu, ffn1_w_down, ffn2_w_gu, ffn2_w_down, cross_wq, cross_wkv,
           cross_wo, hyb_w_in, hyb_w_out, cmp_pe_k, cmp_w1_k, cmp_w2_k, cmp_pe_v, cmp_w1_v, cmp_w2_v,
           ssd_w_in, ssd_conv_w, ssd_conv_b, ssd_dt_bias, ssd_A_log, ssd_D, ssd_norm_g, ssd_w_out):
    b, s, d = x.shape
    depth = norm_g.shape[0]
    h = x
    for i in range(depth):
        g = norm_g[i].astype(F32)[:, None, :]
        j = i // 2
        h = _ffn(h.reshape(b * s, d), g[0], g[1], ffn1_w_gu[i].astype(BF16),
                 ffn1_w_down[i].astype(BF16)).reshape(b, s, d)
        if i % 2 == 0:
            h = _hybrid_layer(h, positions, g[2], g[3],
                              *_prep_hybrid(hyb_w_in[j], hyb_w_out[j], cmp_pe_k[j], cmp_w1_k[j], cmp_w2_k[j],
                                            cmp_pe_v[j], cmp_w1_v[j], cmp_w2_v[j]))
        else:
            h = _ssd_layer(h, g[2], g[3], *_prep_ssd(ssd_w_in[j], ssd_conv_w[j], ssd_conv_b[j], ssd_dt_bias[j],
                                                     ssd_A_log[j], ssd_D[j], ssd_norm_g[j], ssd_w_out[j]))
        k, v = _memkv(mem, g[6], cross_wkv[i].astype(BF16))
        h = _cross(h, g[4], g[5], cross_wq[i].astype(BF16), k, v, cross_wo[i].astype(BF16), CROSS_HEADS)
        h = _ffn(h.reshape(b * s, d), g[7], g[8], ffn2_w_gu[i].astype(BF16),
                 ffn2_w_down[i].astype(BF16)).reshape(b, s, d)
    return h
```

```python
import functools
import math

import jax
import jax.numpy as jnp
import numpy as np
from jax import lax
from jax.experimental import pallas as pl
from jax.experimental.pallas import tpu as pltpu

F32 = jnp.float32
BF16 = jnp.bfloat16

RMS_EPS = 1e-6
ROPE_THETA = 10000.0
HEAD_DIM = 64
ROPE_HALF = HEAD_DIM // 2

V7X_VMEM_BYTES = 64 * 1024 * 1024
VMEM_LIMIT = (V7X_VMEM_BYTES * 3) // 4
LANES = 128

TOKEN_TILE = 512
FF_CHUNK = 512


def _cparams(*sem):
    return pltpu.CompilerParams(dimension_semantics=sem, vmem_limit_bytes=VMEM_LIMIT)


def _rms(x, g):
    ms = jnp.mean(x * x, axis=-1, keepdims=True)
    return x * lax.rsqrt(ms + RMS_EPS) * g


def _dot(a, b):
    return jnp.dot(a, b, preferred_element_type=F32)


def _dot_nt(a, b):
    return lax.dot_general(a, b, (((1,), (1,)), ((), ())), preferred_element_type=F32)


def _const_spec(shape):
    nd = len(shape)
    return pl.BlockSpec(shape, lambda *_: (0,) * nd)


def _ffn_kernel(h_ref, gpre_ref, gpost_ref, wgu_ref, wd_ref, o_ref):
    d_ff = wd_ref.shape[0]
    h = h_ref[...]
    xn = _rms(h, gpre_ref[...]).astype(BF16)
    acc = jnp.zeros(h.shape, F32)
    for c in range(d_ff // FF_CHUNK):
        lo = c * FF_CHUNK
        gate = _dot(xn, wgu_ref[:, lo:lo + FF_CHUNK])
        up = _dot(xn, wgu_ref[:, d_ff + lo:d_ff + lo + FF_CHUNK])
        act = (gate * jax.nn.sigmoid(gate) * up).astype(BF16)
        acc = acc + _dot(act, wd_ref[lo:lo + FF_CHUNK, :])
    o_ref[...] = h + 0.5 * _rms(acc, gpost_ref[...])


def _ffn(h, g_pre, g_post, w_gu, w_down):
    t, d = h.shape
    d_ff = w_down.shape[0]
    tm = min(TOKEN_TILE, t)
    return pl.pallas_call(
        _ffn_kernel,
        grid=(t // tm,),
        in_specs=[
            pl.BlockSpec((tm, d), lambda i: (i, 0)),
            _const_spec((1, d)),
            _const_spec((1, d)),
            _const_spec((d, 2 * d_ff)),
            _const_spec((d_ff, d)),
        ],
        out_specs=pl.BlockSpec((tm, d), lambda i: (i, 0)),
        out_shape=jax.ShapeDtypeStruct((t, d), F32),
        compiler_params=_cparams("parallel"),
        name="ffn",
    )(h, g_pre, g_post, w_gu, w_down)


def _memkv_kernel(mem_ref, g_ref, wkv_ref, k_ref, v_ref):
    d = mem_ref.shape[-1]
    mn = _rms(mem_ref[...], g_ref[...]).astype(BF16)
    kv = _dot(mn, wkv_ref[...])
    k_ref[...] = kv[:, :d].astype(BF16)
    v_ref[...] = kv[:, d:].astype(BF16)


def _memkv(mem, g_mem, wkv):
    b, m, d = mem.shape
    return pl.pallas_call(
        _memkv_kernel,
        grid=(b,),
        in_specs=[
            pl.BlockSpec((None, m, d), lambda i: (i, 0, 0)),
            _const_spec((1, d)),
            _const_spec((d, 2 * d)),
        ],
        out_specs=[pl.BlockSpec((None, m, d), lambda i: (i, 0, 0))] * 2,
        out_shape=[jax.ShapeDtypeStruct((b, m, d), BF16)] * 2,
        compiler_params=_cparams("parallel"),
        name="memkv",
    )(mem, g_mem, wkv)


def _cross_kernel(n_heads, h_ref, gpre_ref, gpost_ref, wq_ref, k_ref, v_ref, wo_ref, o_ref):
    h = h_ref[...]
    d = h.shape[-1]
    hd = d // n_heads
    hn = _rms(h, gpre_ref[...]).astype(BF16)
    q = (_dot(hn, wq_ref[...]) * (hd ** -0.5)).astype(BF16)
    c = jnp.zeros(h.shape, F32)
    for a in range(n_heads):
        sl = slice(a * hd, (a + 1) * hd)
        s = _dot_nt(q[:, sl], k_ref[:, sl])
        e = jnp.exp(s - jnp.max(s, axis=-1, keepdims=True))
        p = e / jnp.sum(e, axis=-1, keepdims=True)
        o = _dot(p.astype(BF16), v_ref[:, sl]).astype(BF16)
        c = c + _dot(o, wo_ref[sl, :])
    o_ref[...] = h + _rms(c, gpost_ref[...])


def _cross(h, g_pre, g_post, wq, k, v, wo, n_heads):
    b, s, d = h.shape
    m = k.shape[1]
    tm = min(TOKEN_TILE, s)
    return pl.pallas_call(
        functools.partial(_cross_kernel, n_heads),
        grid=(b, s // tm),
        in_specs=[
            pl.BlockSpec((None, tm, d), lambda i, j: (i, j, 0)),
            _const_spec((1, d)),
            _const_spec((1, d)),
            _const_spec((d, d)),
            pl.BlockSpec((None, m, d), lambda i, j: (i, 0, 0)),
            pl.BlockSpec((None, m, d), lambda i, j: (i, 0, 0)),
            _const_spec((d, d)),
        ],
        out_specs=pl.BlockSpec((None, tm, d), lambda i, j: (i, j, 0)),
        out_shape=jax.ShapeDtypeStruct((b, s, d), F32),
        compiler_params=_cparams("parallel", "parallel"),
        name="cross",
    )(h, g_pre, g_post, wq, k, v, wo)


def _outproj_kernel(n_in, *refs):
    a_refs = refs[:n_in]
    w_refs = refs[n_in:2 * n_in]
    h_ref, g_ref, o_ref = refs[2 * n_in:]
    m = _dot(a_refs[0][...], w_refs[0][...])
    for a_ref, w_ref in zip(a_refs[1:], w_refs[1:]):
        m = m + _dot(a_ref[...], w_ref[...])
    o_ref[...] = h_ref[...] + _rms(m, g_ref[...])


def _outproj(acts, weights, h, g):
    t, d = h.shape
    tm = min(TOKEN_TILE, t)
    n_in = len(acts)
    return pl.pallas_call(
        functools.partial(_outproj_kernel, n_in),
        grid=(t // tm,),
        in_specs=(
            [pl.BlockSpec((tm, a.shape[1]), lambda i: (i, 0)) for a in acts]
            + [_const_spec(w.shape) for w in weights]
            + [pl.BlockSpec((tm, d), lambda i: (i, 0)), _const_spec((1, d))]
        ),
        out_specs=pl.BlockSpec((tm, d), lambda i: (i, 0)),
        out_shape=jax.ShapeDtypeStruct((t, d), F32),
        compiler_params=_cparams("parallel"),
        name="outproj",
    )(*acts, *weights, h, g)


SSD_HEADDIM = 64
SSD_GROUPS = 4
SSD_STATE = 128
SSD_CONV = 4
SSD_L = 128
CONV_HALO = 8
NEG_BIG = -1e30


def _split3(x):
    a = x.astype(BF16)
    r = x - a.astype(F32)
    b = r.astype(BF16)
    c = (r - b.astype(F32)).astype(BF16)
    return a, b, c


def _softplus(x):
    return jnp.maximum(x, 0.0) + jnp.log(1.0 + jnp.exp(-jnp.abs(x)))


def _ssd_proj_kernel(h_ref, g_ref, w_ref, z_ref, xbc_ref, dt_ref):
    xn = _rms(h_ref[...], g_ref[...]).astype(BF16)
    inner = z_ref.shape[-1]
    cc = xbc_ref.shape[-1]
    for lo in range(0, inner, FF_CHUNK):
        z_ref[:, lo:lo + FF_CHUNK] = _dot(xn, w_ref[:, lo:lo + FF_CHUNK]).astype(BF16)
    for lo in range(0, cc, FF_CHUNK):
        xbc_ref[:, lo:lo + FF_CHUNK] = _dot(xn, w_ref[:, inner + lo:inner + lo + FF_CHUNK]).astype(BF16)
    dt_ref[...] = _dot(xn, w_ref[:, inner + cc:])


def _ssd_proj(h, g, w, inner, cc):
    t, d = h.shape
    tm = min(TOKEN_TILE, t)
    row = lambda n: pl.BlockSpec((tm, n), lambda i: (i, 0))
    return pl.pallas_call(
        _ssd_proj_kernel,
        grid=(t // tm,),
        in_specs=[row(d), _const_spec((1, d)), _const_spec(w.shape)],
        out_specs=[row(inner), row(cc), row(LANES)],
        out_shape=[jax.ShapeDtypeStruct((t, inner), BF16), jax.ShapeDtypeStruct((t, cc), BF16),
                   jax.ShapeDtypeStruct((t, LANES), F32)],
        compiler_params=_cparams("parallel"),
        name="ssd_proj",
    )(h, g, w)


def _ssd_scan_kernel(xbc_ref, dt_ref, z_ref, cw_ref, cb_ref, dtb_ref, alog_ref, dskip_ref, ng_ref,
                     y_ref, buf_ref, xc_ref, state_ref, ys_ref):
    L = xbc_ref.shape[0]
    cc = xbc_ref.shape[1]
    inner = z_ref.shape[1]
    G, N, P = SSD_GROUPS, SSD_STATE, SSD_HEADDIM
    hpg = inner // P // G
    gw = hpg * P
    H0 = CONV_HALO

    @pl.when(pl.program_id(1) == 0)
    def _():
        buf_ref[0:H0, :] = jnp.zeros((H0, cc), F32)
        state_ref[...] = jnp.zeros(state_ref.shape, F32)

    buf_ref[H0:H0 + L, :] = xbc_ref[...].astype(F32)
    for lo in range(0, cc, FF_CHUNK):
        sl = slice(lo, lo + FF_CHUNK)
        acc = cb_ref[:, sl] + cw_ref[0:1, sl] * buf_ref[H0 - 3:H0 - 3 + L, sl]
        for k in range(1, SSD_CONV):
            acc = acc + cw_ref[k:k + 1, sl] * buf_ref[H0 - 3 + k:H0 - 3 + k + L, sl]
        xc_ref[:, sl] = acc * jax.nn.sigmoid(acc)
    buf_ref[0:H0, :] = buf_ref[L:L + H0, :]

    dt = _softplus(dt_ref[...] + dtb_ref[...])
    da = dt * (-jnp.exp(alog_ref[...]))
    ti = lax.broadcasted_iota(jnp.int32, (L, L), 0)
    si = lax.broadcasted_iota(jnp.int32, (L, L), 1)
    causal = si <= ti
    tri = jnp.where(causal, 1.0, 0.0).astype(BF16)
    d1, d2, d3 = _split3(da)
    acum = _dot(tri, d1) + _dot(tri, d2) + _dot(tri, d3)
    acum_t = acum.T
    dt_t = dt.T
    e_acum = jnp.exp(acum)
    last = acum[L - 1:L, :]
    dec_end = jnp.exp(last - acum) * dt
    e_last = jnp.exp(last)
    lane = lax.broadcasted_iota(jnp.int32, (L, 2 * P), 1)
    first = lane < P
    lane1 = lax.broadcasted_iota(jnp.int32, (1, 2 * P), 1)
    first1 = lane1 < P

    for g in range(G):
        bg = xc_ref[:, inner + g * N:inner + (g + 1) * N]
        cg = xc_ref[:, inner + G * N + g * N:inner + G * N + (g + 1) * N]
        cbm = _dot_nt(cg.astype(BF16), bg.astype(BF16))
        dx_parts = []
        sd_parts = []
        for pr in range(hpg // 2):
            h0 = g * hpg + 2 * pr
            xsl = slice(h0 * P, (h0 + 2) * P)
            x_pair = xc_ref[:, xsl]
            st_pair = state_ref[g, :, 2 * pr * P:(2 * pr + 2) * P]
            rhs = jnp.concatenate([x_pair, st_pair], axis=0).astype(BF16)
            ys = []
            for h in (h0, h0 + 1):
                col = acum[:, h:h + 1]
                seg = col - acum_t[h:h + 1, :]
                decay = jnp.exp(jnp.where(causal, seg, NEG_BIG))
                w = cbm * decay * dt_t[h:h + 1, :]
                cs = cg * e_acum[:, h:h + 1]
                lhs = jnp.concatenate([w, cs], axis=1).astype(BF16)
                ys.append(_dot(lhs, rhs))
            y_pair = jnp.where(first, ys[0], ys[1]) + dskip_ref[:, xsl] * x_pair
            zp = z_ref[:, xsl].astype(F32)
            ys_ref[:, xsl] = y_pair * (zp * jax.nn.sigmoid(zp))
            f = jnp.where(first, dec_end[:, h0:h0 + 1], dec_end[:, h0 + 1:h0 + 2])
            dx_parts.append((x_pair * f).astype(BF16))
            sd_parts.append(jnp.where(first1, e_last[:, h0:h0 + 1], e_last[:, h0 + 1:h0 + 2]))
        dx = jnp.concatenate(dx_parts, axis=1)
        sd = jnp.concatenate(sd_parts, axis=1)
        upd = lax.dot_general(bg.astype(BF16), dx, (((0,), (0,)), ((), ())), preferred_element_type=F32)
        state_ref[g] = state_ref[g] * sd + upd

    y_ref[...] = _rms(ys_ref[...], ng_ref[...]).astype(BF16)


def _ssd_scan(xbc, dt, z, conv_w, conv_b, dt_bias, a_log, d_skip, norm_g):
    b, s, cc = xbc.shape
    inner = z.shape[-1]
    L = min(SSD_L, s)
    G, N = SSD_GROUPS, SSD_STATE
    blk = lambda n: pl.BlockSpec((None, L, n), lambda i, j: (i, j, 0))
    return pl.pallas_call(
        _ssd_scan_kernel,
        grid=(b, s // L),
        in_specs=[blk(cc), blk(LANES), blk(inner), _const_spec(conv_w.shape), _const_spec((1, cc)),
                  _const_spec((1, LANES)), _const_spec((1, LANES)), _const_spec((1, inner)),
                  _const_spec((1, inner))],
        out_specs=blk(inner),
        out_shape=jax.ShapeDtypeStruct((b, s, inner), BF16),
        scratch_shapes=[pltpu.VMEM((CONV_HALO + L, cc), F32), pltpu.VMEM((L, cc), F32),
                        pltpu.VMEM((G, N, inner // G), F32), pltpu.VMEM((L, inner), F32)],
        compiler_params=_cparams("parallel", "arbitrary"),
        name="ssd_scan",
    )(xbc, dt, z, conv_w, conv_b, dt_bias, a_log, d_skip, norm_g)


def _pad_lanes(v):
    return jnp.zeros((1, LANES), F32).at[0, :v.shape[0]].set(v.astype(F32))


def _prep_ssd(w_in, conv_w, conv_b, dt_bias, a_log, d_skip, norm_g, w_out):
    d, n_in = w_in.shape
    n_heads = dt_bias.shape[0]
    inner = n_heads * SSD_HEADDIM
    cc = conv_w.shape[1]
    w = jnp.zeros((d, inner + cc + LANES), BF16).at[:, :n_in].set(w_in.astype(BF16))
    return (w, conv_w.astype(F32), conv_b.astype(F32)[None], _pad_lanes(dt_bias), _pad_lanes(a_log),
            jnp.repeat(d_skip.astype(F32), SSD_HEADDIM)[None], norm_g.astype(F32)[None], w_out.astype(BF16))


def _ssd_layer(h, g_pre, g_post, w, conv_w, conv_b, dt_bias, a_log, d_skip, norm_g, w_out):
    b, s, d = h.shape
    inner = norm_g.shape[-1]
    cc = conv_w.shape[1]
    hf = h.reshape(b * s, d)
    z, xbc, dt = _ssd_proj(hf, g_pre, w, inner, cc)
    y = _ssd_scan(xbc.reshape(b, s, cc), dt.reshape(b, s, LANES), z.reshape(b, s, inner),
                  conv_w, conv_b, dt_bias, a_log, d_skip, norm_g)
    return _outproj([y.reshape(b * s, inner)], [w_out], hf, g_post).reshape(b, s, d)


SB_HEADS = 8
NSA_HEADS = 8
NSA_GROUPS = 2
NSA_REP = NSA_HEADS // NSA_GROUPS
CMP_LEN = 32
CMP_STRIDE = 16
SEL_BLOCK = 64
SEL_TOPN = 16
WINDOW = 512
FORCE_BONUS = 1e4
NSA_QB = 128
SB_W = SB_HEADS * HEAD_DIM
NSA_QW = NSA_HEADS * HEAD_DIM
NSA_KVW = NSA_GROUPS * HEAD_DIM
ATT_SCALE = HEAD_DIM ** -0.5

_C_QSB, _C_KSB, _C_VSB = 0, SB_W, 2 * SB_W
_C_QN = 3 * SB_W
_C_QNP = _C_QN + NSA_QW
_C_K3 = _C_QNP + NSA_QW
_C_K3P = _C_K3 + 3 * NSA_KVW
_C_V3 = _C_K3P + 3 * NSA_KVW
_C_GATE = _C_V3 + 3 * NSA_KVW
_C_END = _C_GATE + LANES


def _hyb_proj_kernel(h_ref, g_ref, pos_ref, inv_ref, sgn_ref, w_ref,
                     qsb_ref, ksb_ref, vsb_ref, qn_ref, kcmp_ref, vcmp_ref,
                     ksel_ref, vsel_ref, kwin_ref, vwin_ref, gate_ref):
    xn = _rms(h_ref[...], g_ref[...]).astype(BF16)
    tm = xn.shape[0]
    hd = HEAD_DIM

    def proj(lo, n):
        return _dot(xn, w_ref[:, lo:lo + n])

    for ref, lo, scale in ((qsb_ref, _C_QSB, ATT_SCALE), (ksb_ref, _C_KSB, 1.0), (vsb_ref, _C_VSB, 1.0)):
        p = proj(lo, SB_W) * scale
        for a in range(SB_HEADS):
            ref[a] = p[:, a * hd:(a + 1) * hd].astype(BF16)

    ang = pos_ref[...] * inv_ref[...]
    cos = jnp.cos(ang)
    sin = jnp.sin(ang) * sgn_ref[...]

    def rot(lo, lop, n):
        reps = n // LANES
        return (proj(lo, n) * jnp.concatenate([cos] * reps, axis=1)
                + proj(lop, n) * jnp.concatenate([sin] * reps, axis=1))

    qn = rot(_C_QN, _C_QNP, NSA_QW) * ATT_SCALE
    lane = lax.broadcasted_iota(jnp.int32, (tm, LANES), 1)
    low = lane < hd
    for a in range(NSA_HEADS):
        grp = a // NSA_REP
        piece = qn[:, (a // 2) * LANES:(a // 2 + 1) * LANES]
        if (a % 2) != grp:
            piece = pltpu.roll(piece, hd, 1)
        keep = low if grp == 0 else jnp.logical_not(low)
        qn_ref[a] = jnp.where(keep, piece, 0.0).astype(BF16)

    k3 = rot(_C_K3, _C_K3P, 3 * NSA_KVW)
    v3 = proj(_C_V3, 3 * NSA_KVW)
    for grp in range(NSA_GROUPS):
        kcmp_ref[grp] = k3[:, grp * hd:(grp + 1) * hd].astype(BF16)
        vcmp_ref[grp] = v3[:, grp * hd:(grp + 1) * hd].astype(BF16)
    ksel_ref[...] = k3[:, NSA_KVW:2 * NSA_KVW].astype(BF16)
    vsel_ref[...] = v3[:, NSA_KVW:2 * NSA_KVW].astype(BF16)
    kwin_ref[...] = k3[:, 2 * NSA_KVW:].astype(BF16)
    vwin_ref[...] = v3[:, 2 * NSA_KVW:].astype(BF16)
    gate_ref[...] = jax.nn.sigmoid(proj(_C_GATE, LANES))


def _hyb_proj(h, g, posf, inv, sgn, w):
    b, s, d = h.shape
    tm = min(TOKEN_TILE, s)
    heads = lambda n, w_: pl.BlockSpec((None, n, tm, w_), lambda i, j: (i, 0, j, 0))
    rows = lambda w_: pl.BlockSpec((None, tm, w_), lambda i, j: (i, j, 0))
    hshape = lambda n, w_: jax.ShapeDtypeStruct((b, n, s, w_), BF16)
    rshape = lambda w_, dt: jax.ShapeDtypeStruct((b, s, w_), dt)
    return pl.pallas_call(
        _hyb_proj_kernel,
        grid=(b, s // tm),
        in_specs=[rows(d), _const_spec((1, d)), rows(1), _const_spec((1, LANES)), _const_spec((1, LANES)),
                  _const_spec(w.shape)],
        out_specs=[heads(SB_HEADS, HEAD_DIM)] * 3 + [heads(NSA_HEADS, LANES)]
        + [heads(NSA_GROUPS, HEAD_DIM)] * 2 + [rows(LANES)] * 5,
        out_shape=[hshape(SB_HEADS, HEAD_DIM)] * 3 + [hshape(NSA_HEADS, LANES)]
        + [hshape(NSA_GROUPS, HEAD_DIM)] * 2 + [rshape(LANES, BF16)] * 4 + [rshape(LANES, F32)],
        compiler_params=_cparams("parallel", "parallel"),
        name="hyb_proj",
    )(h, g, posf, inv, sgn, w)


SB_TILE = 512
SB_SUB = 256
SB_DEAD_LOG = -110.0


def _sb_kernel(qi_ref, kj_ref, q_ref, k_ref, v_ref, o_ref, acc_ref, carry_ref, live_ref):
    p = pl.program_id(1)
    i = qi_ref[p]
    j = kj_ref[p]
    n_heads, tq, hd = q_ref.shape
    tk = k_ref.shape[1]
    sub = min(SB_SUB, tk)

    @pl.when(j == i)
    def _():
        acc_ref[...] = jnp.zeros(acc_ref.shape, F32)
        carry_ref[...] = jnp.zeros(carry_ref.shape, F32)
        live_ref[0] = 1

    kr = lax.broadcasted_iota(jnp.int32, (sub, sub), 0)
    kc = lax.broadcasted_iota(jnp.int32, (sub, sub), 1)
    later = jnp.where(kr > kc, 1.0, 0.0).astype(BF16)

    def sub_block(diag, sb):
        r0 = sb * sub if diag else 0
        rows = tq - r0
        for a in range(n_heads):
            q = q_ref[a, r0:, :]
            k = k_ref[a, sb * sub:(sb + 1) * sub, :]
            v = v_ref[a, sb * sub:(sb + 1) * sub, :]
            z = _dot_nt(q, k)
            ls = -(jnp.maximum(z, 0.0) + jnp.log(1.0 + jnp.exp(-jnp.abs(z))))
            if diag:
                qpos = lax.broadcasted_iota(jnp.int32, (rows, sub), 0)
                kpos = lax.broadcasted_iota(jnp.int32, (rows, sub), 1)
                mask = kpos < qpos
                ls = jnp.where(mask, ls, 0.0)
            hi = ls.astype(BF16)
            lo = (ls - hi.astype(F32)).astype(BF16)
            cs = _dot(hi, later) + _dot(lo, later)
            carry = carry_ref[a, r0:, :]
            w = jnp.exp(z + ls + cs + carry)
            if diag:
                w = jnp.where(mask, w, 0.0)
            acc_ref[a, r0:, :] += _dot(w.astype(BF16), v)
            carry_ref[a, r0:, :] = carry + cs[:, 0:1] + ls[:, 0:1]
        live_ref[0] = (jnp.max(carry_ref[...]) > SB_DEAD_LOG).astype(jnp.int32)

    def tile(diag):
        for sb in reversed(range(tk // sub)):
            pl.when(live_ref[0] == 1)(functools.partial(sub_block, diag, sb))

    pl.when(j == i)(lambda: tile(True))
    pl.when(j < i)(lambda: tile(False))

    @pl.when(j == 0)
    def _():
        for a in range(n_heads):
            o_ref[:, a * hd:(a + 1) * hd] = acc_ref[a].astype(o_ref.dtype)


def _sb_attn(q, k, v):
    b, n_heads, s, hd = q.shape
    t = min(SB_TILE, s)
    nq = s // t
    pairs = [(i, j) for i in range(nq) for j in range(i, -1, -1)]
    qi = jnp.asarray([p[0] for p in pairs], jnp.int32)
    kj = jnp.asarray([p[1] for p in pairs], jnp.int32)
    grid_spec = pltpu.PrefetchScalarGridSpec(
        num_scalar_prefetch=2,
        grid=(b, len(pairs)),
        in_specs=[
            pl.BlockSpec((None, n_heads, t, hd), lambda bi, p, qi, kj: (bi, 0, qi[p], 0)),
            pl.BlockSpec((None, n_heads, t, hd), lambda bi, p, qi, kj: (bi, 0, kj[p], 0)),
            pl.BlockSpec((None, n_heads, t, hd), lambda bi, p, qi, kj: (bi, 0, kj[p], 0)),
        ],
        out_specs=pl.BlockSpec((None, t, n_heads * hd), lambda bi, p, qi, kj: (bi, qi[p], 0)),
        scratch_shapes=[pltpu.VMEM((n_heads, t, hd), F32), pltpu.VMEM((n_heads, t, 1), F32),
                        pltpu.SMEM((1,), jnp.int32)],
    )
    return pl.pallas_call(
        _sb_kernel,
        grid_spec=grid_spec,
        out_shape=jax.ShapeDtypeStruct((b, s, n_heads * hd), BF16),
        compiler_params=_cparams("parallel", "arbitrary"),
        name="sb_attn",
    )(qi, kj, q, k, v)


def _compress_kernel(t_ref, wab_ref, pe_ref, w1_ref, w2_ref, o_ref):
    n_grp, nch, _ = t_ref.shape
    hid = w2_ref.shape[0]
    pe8 = jnp.broadcast_to(pe_ref[...], (8, pe_ref.shape[1])).astype(BF16)
    bias = _dot(pe8, w1_ref[...])[0:1, :]
    for grp in range(n_grp):
        ab = _dot(t_ref[grp], wab_ref[...])
        nxt = pltpu.roll(ab[:, hid:], nch - 1, 0)
        hcur = ab[:, :hid] + nxt + bias
        act = (hcur * jax.nn.sigmoid(hcur)).astype(BF16)
        o_ref[:, grp * HEAD_DIM:(grp + 1) * HEAD_DIM] = _dot(act, w2_ref[...]).astype(o_ref.dtype)


def _compress(t, wab, pe, w1, w2):
    b, n_grp, nch, kw = t.shape
    return pl.pallas_call(
        _compress_kernel,
        grid=(b,),
        in_specs=[pl.BlockSpec((None, n_grp, nch, kw), lambda i: (i, 0, 0, 0)), _const_spec(wab.shape),
                  _const_spec(pe.shape), _const_spec(w1.shape), _const_spec(w2.shape)],
        out_specs=pl.BlockSpec((None, nch, n_grp * HEAD_DIM), lambda i: (i, 0, 0)),
        out_shape=jax.ShapeDtypeStruct((b, nch, n_grp * HEAD_DIM), BF16),
        compiler_params=_cparams("parallel"),
        name="nsa_compress",
    )(t, wab, pe, w1, w2)


def _masked_softmax_rows(s, mask):
    sm = jnp.where(mask, s, NEG_BIG)
    m = jnp.max(sm, axis=-1, keepdims=True)
    e = jnp.where(mask, jnp.exp(sm - m), 0.0)
    den = jnp.sum(e, axis=-1, keepdims=True)
    return e / jnp.maximum(den, 1e-30)


def _select_kernel(qf_ref, kc_ref, o_ref):
    n_heads, nsub, _ = qf_ref.shape
    nch = kc_ref.shape[0]
    q = qf_ref[...].reshape(n_heads * nsub, LANES)
    s = _dot_nt(q, kc_ref[...])
    sub_id = lax.broadcasted_iota(jnp.int32, (n_heads * nsub, nch), 0) % nsub
    c_id = lax.broadcasted_iota(jnp.int32, (n_heads * nsub, nch), 1)
    p = _masked_softmax_rows(s, c_id * CMP_STRIDE + (CMP_LEN - 1) <= sub_id * SEL_BLOCK)
    rows = NSA_GROUPS * nsub
    psum = jnp.concatenate(
        [sum(p[(grp * NSA_REP + r) * nsub:(grp * NSA_REP + r + 1) * nsub] for r in range(NSA_REP))
         for grp in range(NSA_GROUPS)], axis=0)
    oc = lax.broadcasted_iota(jnp.int32, (nch, LANES), 0) * CMP_STRIDE
    oj = lax.broadcasted_iota(jnp.int32, (nch, LANES), 1) * SEL_BLOCK
    overlap = jnp.where((oc < oj + SEL_BLOCK) & (oc + (CMP_LEN - 1) >= oj), 1.0, 0.0).astype(BF16)
    p1, p2, p3 = _split3(psum)
    imp = _dot(p1, overlap) + _dot(p2, overlap) + _dot(p3, overlap)
    jl = lax.broadcasted_iota(jnp.int32, (rows, LANES), 1)
    cur = lax.broadcasted_iota(jnp.int32, (rows, LANES), 0) % nsub
    forced = (jl == 0) | (jl == cur) | (jl == cur - 1)
    score = jnp.where(jl <= cur, imp + jnp.where(forced, FORCE_BONUS, 0.0), -1.0)

    def body(d, rank):
        other = pltpu.roll(score, d, 1)
        ahead = (other > score) | ((other == score) & (jl >= d))
        return rank + jnp.where(ahead, 1.0, 0.0)

    rank = lax.fori_loop(1, LANES, body, jnp.zeros((rows, LANES), F32))
    jf = jl.astype(F32)
    out = jnp.zeros((rows, LANES), F32)
    for r in range(SEL_TOPN):
        col = jnp.sum(jnp.where(rank == float(r), jf, 0.0), axis=-1, keepdims=True)
        out = jnp.where(jl == r, col, out)
    o_ref[...] = out.astype(jnp.int32)


def _nsa_select(qf, kc):
    b, n_heads, nsub, _ = qf.shape
    nch = kc.shape[1]
    assert SEL_TOPN <= nsub <= LANES
    return pl.pallas_call(
        _select_kernel,
        grid=(b,),
        in_specs=[pl.BlockSpec((None, n_heads, nsub, LANES), lambda i: (i, 0, 0, 0)),
                  pl.BlockSpec((None, nch, LANES), lambda i: (i, 0, 0))],
        out_specs=pl.BlockSpec((None, NSA_GROUPS * nsub, LANES), lambda i: (i, 0, 0)),
        out_shape=jax.ShapeDtypeStruct((b, NSA_GROUPS * nsub, LANES), jnp.int32),
        compiler_params=_cparams("parallel"),
        name="nsa_select",
    )(qf, kc)


def _nsa_kernel(sel_ref, qn_ref, kc_ref, vc_ref, ksel_ref, vsel_ref, kwin_ref, vwin_ref, gate_ref,
                o_ref, kg_ref, vg_ref):
    bi = pl.program_id(0)
    i = pl.program_id(1)
    n_heads, qb, _ = qn_ref.shape
    s_len = ksel_ref.shape[0]
    nsub_total = s_len // SEL_BLOCK
    nch = kc_ref.shape[0]
    hd = HEAD_DIM
    q = qn_ref[...].reshape(n_heads * qb, LANES)
    qpos = i * qb + lax.broadcasted_iota(jnp.int32, (n_heads * qb, 1), 0) % qb

    c_end = lax.broadcasted_iota(jnp.int32, (1, nch), 1) * CMP_STRIDE + (CMP_LEN - 1)
    p_c = _masked_softmax_rows(_dot_nt(q, kc_ref[...]), c_end <= qpos)
    o_c = _dot(p_c.astype(BF16), vc_ref[...])

    wlen = WINDOW + qb
    start = pl.multiple_of(jnp.maximum(i * qb - WINDOW, 0), qb)
    kpos = start + lax.broadcasted_iota(jnp.int32, (1, wlen), 1)
    delta = qpos - kpos
    p_w = _masked_softmax_rows(_dot_nt(q, kwin_ref[pl.ds(start, wlen), :]), (delta >= 0) & (delta < WINDOW))
    o_w = _dot(p_w.astype(BF16), vwin_ref[pl.ds(start, wlen), :])

    nsel = SEL_TOPN * SEL_BLOCK
    lane_blk = lax.broadcasted_iota(jnp.int32, (1, nsel), 1) // SEL_BLOCK
    lane_off = lax.broadcasted_iota(jnp.int32, (1, nsel), 1) % SEL_BLOCK
    n_sub = qb // SEL_BLOCK
    o_s = [[None] * n_sub for _ in range(NSA_GROUPS)]
    for grp in range(NSA_GROUPS):
        for n in range(n_sub):
            base = ((bi * NSA_GROUPS + grp) * nsub_total + i * n_sub + n) * SEL_TOPN
            tok = lane_off
            for t in range(SEL_TOPN):
                idx = sel_ref[base + t]
                off = pl.multiple_of(idx * SEL_BLOCK, SEL_BLOCK)
                kg_ref[t * SEL_BLOCK:(t + 1) * SEL_BLOCK, :] = ksel_ref[pl.ds(off, SEL_BLOCK), :]
                vg_ref[t * SEL_BLOCK:(t + 1) * SEL_BLOCK, :] = vsel_ref[pl.ds(off, SEL_BLOCK), :]
                tok = tok + jnp.where(lane_blk == t, idx * SEL_BLOCK, 0)
            qs = qn_ref[grp * NSA_REP:(grp + 1) * NSA_REP, n * SEL_BLOCK:(n + 1) * SEL_BLOCK, :]
            qs = qs.reshape(NSA_REP * SEL_BLOCK, LANES)
            qp = (i * qb + n * SEL_BLOCK
                  + lax.broadcasted_iota(jnp.int32, (NSA_REP * SEL_BLOCK, 1), 0) % SEL_BLOCK)
            p_s = _masked_softmax_rows(_dot_nt(qs, kg_ref[...]), tok <= qp)
            o_s[grp][n] = _dot(p_s.astype(BF16), vg_ref[...])

    lane = lax.broadcasted_iota(jnp.int32, (qb, LANES), 1)
    low = lane < hd
    gates = gate_ref[...]
    mixed = []
    for a in range(n_heads):
        grp, r = a // NSA_REP, a % NSA_REP
        rs = slice(a * qb, (a + 1) * qb)
        sel_rows = jnp.concatenate([o_s[grp][n][r * SEL_BLOCK:(r + 1) * SEL_BLOCK] for n in range(n_sub)],
                                   axis=0)
        mixed.append(gates[:, 3 * a:3 * a + 1] * o_c[rs] + gates[:, 3 * a + 1:3 * a + 2] * sel_rows
                     + gates[:, 3 * a + 2:3 * a + 3] * o_w[rs])
    for pr in range(n_heads // 2):
        grp = (2 * pr) // NSA_REP
        left, right = mixed[2 * pr], mixed[2 * pr + 1]
        if grp == 0:
            right = pltpu.roll(right, hd, 1)
        else:
            left = pltpu.roll(left, hd, 1)
        o_ref[:, pr * LANES:(pr + 1) * LANES] = jnp.where(low, left, right).astype(o_ref.dtype)


def _nsa_attn(sel, qn, kc, vc, ksel, vsel, kwin, vwin, gates):
    b, n_heads, s, _ = qn.shape
    nch = kc.shape[1]
    qb = min(NSA_QB, s)
    assert s >= WINDOW + qb
    full = lambda n: pl.BlockSpec((None, n, LANES), lambda bi, i, sel: (bi, 0, 0))
    grid_spec = pltpu.PrefetchScalarGridSpec(
        num_scalar_prefetch=1,
        grid=(b, s // qb),
        in_specs=[pl.BlockSpec((None, n_heads, qb, LANES), lambda bi, i, sel: (bi, 0, i, 0)),
                  full(nch), full(nch), full(s), full(s), full(s), full(s),
                  pl.BlockSpec((None, qb, LANES), lambda bi, i, sel: (bi, i, 0))],
        out_specs=pl.BlockSpec((None, qb, n_heads * HEAD_DIM), lambda bi, i, sel: (bi, i, 0)),
        scratch_shapes=[pltpu.VMEM((SEL_TOPN * SEL_BLOCK, LANES), BF16)] * 2,
    )
    return pl.pallas_call(
        _nsa_kernel,
        grid_spec=grid_spec,
        out_shape=jax.ShapeDtypeStruct((b, s, n_heads * HEAD_DIM), BF16),
        compiler_params=_cparams("parallel", "arbitrary"),
        name="nsa_attn",
    )(sel, qn, kc, vc, ksel, vsel, kwin, vwin, gates)


def _prep_hybrid(w_in, w_out, pe_k, w1_k, w2_k, pe_v, w1_v, w2_v):
    d = w_in.shape[0]
    hd = HEAD_DIM
    col = np.arange(w_in.shape[1])
    o_qn = 3 * SB_W
    o_kv = o_qn + NSA_QW
    kv = lambda n: col[o_kv + n * NSA_KVW:o_kv + (n + 1) * NSA_KVW]
    partner = lambda c: (c // hd) * hd + (c % hd + hd // 2) % hd
    qn_c = col[o_qn:o_qn + NSA_QW]
    k3_c = np.concatenate([kv(0), kv(2), kv(4)])
    v3_c = np.concatenate([kv(1), kv(3), kv(5)])
    gate_c = col[o_kv + 6 * NSA_KVW:]
    order = np.concatenate([col[:o_qn], qn_c, partner(qn_c), k3_c, partner(k3_c), v3_c, gate_c])
    w = jnp.zeros((d, _C_END), BF16).at[:, :order.shape[0]].set(w_in[:, order].astype(BF16))
    lanes = np.arange(LANES)
    inv = (ROPE_THETA ** (-(lanes % ROPE_HALF).astype(np.float32) / ROPE_HALF)).astype(np.float32)[None]
    sgn = np.where(lanes % hd < ROPE_HALF, -1.0, 1.0).astype(np.float32)[None]

    def cmp_w(pe, w1, w2):
        half = w1.shape[0] // 2
        wab = jnp.concatenate([w1[:half], w1[half:]], axis=1).astype(BF16)
        return wab, pe.reshape(1, -1).astype(F32), w1.astype(BF16), w2.astype(BF16)

    return (w, jnp.asarray(inv), jnp.asarray(sgn), w_out[:SB_W].astype(BF16), w_out[SB_W:].astype(BF16),
            cmp_w(pe_k, w1_k, w2_k), cmp_w(pe_v, w1_v, w2_v))


def _hybrid_layer(h, positions, g_pre, g_post, w, inv, sgn, wo_sb, wo_nsa, cmp_k, cmp_v):
    b, s, d = h.shape
    posf = positions.astype(F32)[..., None]
    (qsb, ksb, vsb, qn, kcmp, vcmp, ksel, vsel, kwin, vwin, gates) = _hyb_proj(h, g_pre, posf, inv, sgn, w)
    o_sb = _sb_attn(qsb, ksb, vsb)
    half = CMP_LEN // 2
    chunks = lambda t: t.reshape(b, NSA_GROUPS, s // half, half * HEAD_DIM)
    kc = _compress(chunks(kcmp), *cmp_k)
    vc = _compress(chunks(vcmp), *cmp_v)
    sel = _nsa_select(qn[:, :, ::SEL_BLOCK, :], kc)
    sel = sel[:, :, :SEL_TOPN].reshape(-1)
    o_nsa = _nsa_attn(sel, qn, kc, vc, ksel, vsel, kwin, vwin, gates)
    hf = h.reshape(b * s, d)
    return _outproj([o_sb.reshape(b * s, SB_W), o_nsa.reshape(b * s, NSA_QW)], [wo_sb, wo_nsa], hf,
                    g_post).reshape(b, s, d)


CROSS_HEADS = 4


def kernel(x, mem, positions, norm_g, ffn1_w_gu, ffn1_w_down, ffn2_w_gu, ffn2_w_down, cross_wq, cross_wkv,
           cross_wo, hyb_w_in, hyb_w_out, cmp_pe_k, cmp_w1_k, cmp_w2_k, cmp_pe_v, cmp_w1_v, cmp_w2_v,
           ssd_w_in, ssd_conv_w, ssd_conv_b, ssd_dt_bias, ssd_A_log, ssd_D, ssd_norm_g, ssd_w_out):
    b, s, d = x.shape
    depth = norm_g.shape[0]
    h = x
    for i in range(depth):
        g = norm_g[i].astype(F32)[:, None, :]
        j = i // 2
        h = _ffn(h.reshape(b * s, d), g[0], g[1], ffn1_w_gu[i].astype(BF16),
                 ffn1_w_down[i].astype(BF16)).reshape(b, s, d)
        if i % 2 == 0:
            h = _hybrid_layer(h, positions, g[2], g[3],
                              *_prep_hybrid(hyb_w_in[j], hyb_w_out[j], cmp_pe_k[j], cmp_w1_k[j], cmp_w2_k[j],
                                            cmp_pe_v[j], cmp_w1_v[j], cmp_w2_v[j]))
        else:
            h = _ssd_layer(h, g[2], g[3], *_prep_ssd(ssd_w_in[j], ssd_conv_w[j], ssd_conv_b[j], ssd_dt_bias[j],
                                                     ssd_A_log[j], ssd_D[j], ssd_norm_g[j], ssd_w_out[j]))
        k, v = _memkv(mem, g[6], cross_wkv[i].astype(BF16))
        h = _cross(h, g[4], g[5], cross_wq[i].astype(BF16), k, v, cross_wo[i].astype(BF16), CROSS_HEADS)
        h = _ffn(h.reshape(b * s, d), g[7], g[8], ffn2_w_gu[i].astype(BF16),
                 ffn2_w_down[i].astype(BF16)).reshape(b, s, d)
    return h
```

```python
import functools
import math

import jax
import jax.numpy as jnp
import numpy as np
from jax import lax
from jax.experimental import pallas as pl
from jax.experimental.pallas import tpu as pltpu

F32 = jnp.float32
BF16 = jnp.bfloat16

RMS_EPS = 1e-6
ROPE_THETA = 10000.0
HEAD_DIM = 64
ROPE_HALF = HEAD_DIM // 2

V7X_VMEM_BYTES = 64 * 1024 * 1024
VMEM_LIMIT = (V7X_VMEM_BYTES * 3) // 4
LANES = 128

TOKEN_TILE = 512
FF_CHUNK = 512


def _cparams(*sem):
    return pltpu.CompilerParams(dimension_semantics=sem, vmem_limit_bytes=VMEM_LIMIT)


def _rms(x, g):
    ms = jnp.mean(x * x, axis=-1, keepdims=True)
    return x * lax.rsqrt(ms + RMS_EPS) * g


def _dot(a, b):
    return jnp.dot(a, b, preferred_element_type=F32)


def _dot_nt(a, b):
    return lax.dot_general(a, b, (((1,), (1,)), ((), ())), preferred_element_type=F32)


def _const_spec(shape):
    nd = len(shape)
    return pl.BlockSpec(shape, lambda *_: (0,) * nd)


def _ffn_kernel(h_ref, gpre_ref, gpost_ref, wgu_ref, wd_ref, o_ref):
    d_ff = wd_ref.shape[0]
    h = h_ref[...]
    xn = _rms(h, gpre_ref[...]).astype(BF16)
    acc = jnp.zeros(h.shape, F32)
    for c in range(d_ff // FF_CHUNK):
        lo = c * FF_CHUNK
        gate = _dot(xn, wgu_ref[:, lo:lo + FF_CHUNK])
        up = _dot(xn, wgu_ref[:, d_ff + lo:d_ff + lo + FF_CHUNK])
        act = (gate * jax.nn.sigmoid(gate) * up).astype(BF16)
        acc = acc + _dot(act, wd_ref[lo:lo + FF_CHUNK, :])
    o_ref[...] = h + 0.5 * _rms(acc, gpost_ref[...])


def _ffn(h, g_pre, g_post, w_gu, w_down):
    t, d = h.shape
    d_ff = w_down.shape[0]
    tm = min(TOKEN_TILE, t)
    return pl.pallas_call(
        _ffn_kernel,
        grid=(t // tm,),
        in_specs=[
            pl.BlockSpec((tm, d), lambda i: (i, 0)),
            _const_spec((1, d)),
            _const_spec((1, d)),
            _const_spec((d, 2 * d_ff)),
            _const_spec((d_ff, d)),
        ],
        out_specs=pl.BlockSpec((tm, d), lambda i: (i, 0)),
        out_shape=jax.ShapeDtypeStruct((t, d), F32),
        compiler_params=_cparams("parallel"),
        name="ffn",
    )(h, g_pre, g_post, w_gu, w_down)


def _memkv_kernel(mem_ref, g_ref, wkv_ref, k_ref, v_ref):
    d = mem_ref.shape[-1]
    mn = _rms(mem_ref[...], g_ref[...]).astype(BF16)
    kv = _dot(mn, wkv_ref[...])
    k_ref[...] = kv[:, :d].astype(BF16)
    v_ref[...] = kv[:, d:].astype(BF16)


def _memkv(mem, g_mem, wkv):
    b, m, d = mem.shape
    return pl.pallas_call(
        _memkv_kernel,
        grid=(b,),
        in_specs=[
            pl.BlockSpec((None, m, d), lambda i: (i, 0, 0)),
            _const_spec((1, d)),
            _const_spec((d, 2 * d)),
        ],
        out_specs=[pl.BlockSpec((None, m, d), lambda i: (i, 0, 0))] * 2,
        out_shape=[jax.ShapeDtypeStruct((b, m, d), BF16)] * 2,
        compiler_params=_cparams("parallel"),
        name="memkv",
    )(mem, g_mem, wkv)


def _cross_kernel(n_heads, h_ref, gpre_ref, gpost_ref, wq_ref, k_ref, v_ref, wo_ref, o_ref):
    h = h_ref[...]
    d = h.shape[-1]
    hd = d // n_heads
    hn = _rms(h, gpre_ref[...]).astype(BF16)
    q = (_dot(hn, wq_ref[...]) * (hd ** -0.5 * LOG2E)).astype(BF16)
    sls = [slice(a * hd, (a + 1) * hd) for a in range(n_heads)]
    ss = [_dot_nt(q[:, sl], k_ref[:, sl]) for sl in sls]
    es = [jnp.exp2(s - jnp.max(s, axis=-1, keepdims=True)) for s in ss]
    os_ = [(_dot(e.astype(BF16), v_ref[:, sl]) / jnp.sum(e, axis=-1, keepdims=True)).astype(BF16)
           for e, sl in zip(es, sls)]
    c = _dot(jnp.concatenate(os_, axis=1), wo_ref[...])
    o_ref[...] = h + _rms(c, gpost_ref[...])


def _cross(h, g_pre, g_post, wq, k, v, wo, n_heads):
    b, s, d = h.shape
    m = k.shape[1]
    tm = min(TOKEN_TILE, s)
    return pl.pallas_call(
        functools.partial(_cross_kernel, n_heads),
        grid=(b, s // tm),
        in_specs=[
            pl.BlockSpec((None, tm, d), lambda i, j: (i, j, 0)),
            _const_spec((1, d)),
            _const_spec((1, d)),
            _const_spec((d, d)),
            pl.BlockSpec((None, m, d), lambda i, j: (i, 0, 0)),
            pl.BlockSpec((None, m, d), lambda i, j: (i, 0, 0)),
            _const_spec((d, d)),
        ],
        out_specs=pl.BlockSpec((None, tm, d), lambda i, j: (i, j, 0)),
        out_shape=jax.ShapeDtypeStruct((b, s, d), F32),
        compiler_params=_cparams("parallel", "parallel"),
        name="cross",
    )(h, g_pre, g_post, wq, k, v, wo)


def _outproj_kernel(n_in, *refs):
    a_refs = refs[:n_in]
    w_refs = refs[n_in:2 * n_in]
    h_ref, g_ref, o_ref = refs[2 * n_in:]
    m = _dot(a_refs[0][...], w_refs[0][...])
    for a_ref, w_ref in zip(a_refs[1:], w_refs[1:]):
        m = m + _dot(a_ref[...], w_ref[...])
    o_ref[...] = h_ref[...] + _rms(m, g_ref[...])


def _outproj(acts, weights, h, g):
    t, d = h.shape
    tm = min(TOKEN_TILE, t)
    n_in = len(acts)
    return pl.pallas_call(
        functools.partial(_outproj_kernel, n_in),
        grid=(t // tm,),
        in_specs=(
            [pl.BlockSpec((tm, a.shape[1]), lambda i: (i, 0)) for a in acts]
            + [_const_spec(w.shape) for w in weights]
            + [pl.BlockSpec((tm, d), lambda i: (i, 0)), _const_spec((1, d))]
        ),
        out_specs=pl.BlockSpec((tm, d), lambda i: (i, 0)),
        out_shape=jax.ShapeDtypeStruct((t, d), F32),
        compiler_params=_cparams("parallel"),
        name="outproj",
    )(*acts, *weights, h, g)


SSD_HEADDIM = 64
SSD_GROUPS = 4
SSD_STATE = 128
SSD_CONV = 4
SSD_L = 128
CONV_HALO = 8
NEG_BIG = -1e30


def _split3(x):
    a = x.astype(BF16)
    r = x - a.astype(F32)
    b = r.astype(BF16)
    c = (r - b.astype(F32)).astype(BF16)
    return a, b, c


def _softplus(x):
    return jnp.maximum(x, 0.0) + jnp.log(1.0 + jnp.exp(-jnp.abs(x)))


def _ssd_proj_kernel(h_ref, g_ref, w_ref, z_ref, xbc_ref, dt_ref):
    xn = _rms(h_ref[...], g_ref[...]).astype(BF16)
    inner = z_ref.shape[-1]
    cc = xbc_ref.shape[-1]
    for lo in range(0, inner, FF_CHUNK):
        z_ref[:, lo:lo + FF_CHUNK] = _dot(xn, w_ref[:, lo:lo + FF_CHUNK]).astype(BF16)
    for lo in range(0, cc, FF_CHUNK):
        xbc_ref[:, lo:lo + FF_CHUNK] = _dot(xn, w_ref[:, inner + lo:inner + lo + FF_CHUNK]).astype(BF16)
    dt_ref[...] = _dot(xn, w_ref[:, inner + cc:])


def _ssd_proj(h, g, w, inner, cc):
    t, d = h.shape
    tm = min(TOKEN_TILE, t)
    row = lambda n: pl.BlockSpec((tm, n), lambda i: (i, 0))
    return pl.pallas_call(
        _ssd_proj_kernel,
        grid=(t // tm,),
        in_specs=[row(d), _const_spec((1, d)), _const_spec(w.shape)],
        out_specs=[row(inner), row(cc), row(LANES)],
        out_shape=[jax.ShapeDtypeStruct((t, inner), BF16), jax.ShapeDtypeStruct((t, cc), BF16),
                   jax.ShapeDtypeStruct((t, LANES), F32)],
        compiler_params=_cparams("parallel"),
        name="ssd_proj",
    )(h, g, w)


def _ssd_scan_kernel(xbc_ref, dt_ref, z_ref, cw_ref, cb_ref, dtb_ref, alog_ref, dskip_ref, ng_ref,
                     y_ref, buf_ref, xc_ref, state_ref, ys_ref):
    L = xbc_ref.shape[0]
    cc = xbc_ref.shape[1]
    inner = z_ref.shape[1]
    G, N, P = SSD_GROUPS, SSD_STATE, SSD_HEADDIM
    hpg = inner // P // G
    gw = hpg * P
    H0 = CONV_HALO

    @pl.when(pl.program_id(1) == 0)
    def _():
        buf_ref[0:H0, :] = jnp.zeros((H0, cc), F32)
        state_ref[...] = jnp.zeros(state_ref.shape, F32)

    buf_ref[H0:H0 + L, :] = xbc_ref[...].astype(F32)
    for lo in range(0, cc, FF_CHUNK):
        sl = slice(lo, lo + FF_CHUNK)
        acc = cb_ref[:, sl] + cw_ref[0:1, sl] * buf_ref[H0 - 3:H0 - 3 + L, sl]
        for k in range(1, SSD_CONV):
            acc = acc + cw_ref[k:k + 1, sl] * buf_ref[H0 - 3 + k:H0 - 3 + k + L, sl]
        xc_ref[:, sl] = acc * jax.nn.sigmoid(acc)
    buf_ref[0:H0, :] = buf_ref[L:L + H0, :]

    dt = _softplus(dt_ref[...] + dtb_ref[...])
    da = dt * (-jnp.exp(alog_ref[...]))
    ti = lax.broadcasted_iota(jnp.int32, (L, L), 0)
    si = lax.broadcasted_iota(jnp.int32, (L, L), 1)
    causal = si <= ti
    tri = jnp.where(causal, 1.0, 0.0).astype(BF16)
    d1, d2, d3 = _split3(da)
    acum = _dot(tri, d1) + _dot(tri, d2) + _dot(tri, d3)
    acum_t = acum.T
    dt_t = dt.T
    e_acum = jnp.exp(acum)
    last = acum[L - 1:L, :]
    dec_end = jnp.exp(last - acum) * dt
    e_last = jnp.exp(last)
    lane = lax.broadcasted_iota(jnp.int32, (L, 2 * P), 1)
    first = lane < P
    lane1 = lax.broadcasted_iota(jnp.int32, (1, 2 * P), 1)
    first1 = lane1 < P

    for g in range(G):
        bg = xc_ref[:, inner + g * N:inner + (g + 1) * N]
        cg = xc_ref[:, inner + G * N + g * N:inner + G * N + (g + 1) * N]
        cbm = _dot_nt(cg.astype(BF16), bg.astype(BF16))
        dx_parts = []
        sd_parts = []
        for pr in range(hpg // 2):
            h0 = g * hpg + 2 * pr
            xsl = slice(h0 * P, (h0 + 2) * P)
            x_pair = xc_ref[:, xsl]
            st_pair = state_ref[g, :, 2 * pr * P:(2 * pr + 2) * P]
            rhs = jnp.concatenate([x_pair, st_pair], axis=0).astype(BF16)
            ys = []
            for h in (h0, h0 + 1):
                col = acum[:, h:h + 1]
                seg = col - acum_t[h:h + 1, :]
                decay = jnp.exp(jnp.where(causal, seg, NEG_BIG))
                w = cbm * decay * dt_t[h:h + 1, :]
                cs = cg * e_acum[:, h:h + 1]
                lhs = jnp.concatenate([w, cs], axis=1).astype(BF16)
                ys.append(_dot(lhs, rhs))
            y_pair = jnp.where(first, ys[0], ys[1]) + dskip_ref[:, xsl] * x_pair
            zp = z_ref[:, xsl].astype(F32)
            ys_ref[:, xsl] = y_pair * (zp * jax.nn.sigmoid(zp))
            f = jnp.where(first, dec_end[:, h0:h0 + 1], dec_end[:, h0 + 1:h0 + 2])
            dx_parts.append((x_pair * f).astype(BF16))
            sd_parts.append(jnp.where(first1, e_last[:, h0:h0 + 1], e_last[:, h0 + 1:h0 + 2]))
        dx = jnp.concatenate(dx_parts, axis=1)
        sd = jnp.concatenate(sd_parts, axis=1)
        upd = lax.dot_general(bg.astype(BF16), dx, (((0,), (0,)), ((), ())), preferred_element_type=F32)
        state_ref[g] = state_ref[g] * sd + upd

    y_ref[...] = _rms(ys_ref[...], ng_ref[...]).astype(BF16)


def _ssd_scan(xbc, dt, z, conv_w, conv_b, dt_bias, a_log, d_skip, norm_g):
    b, s, cc = xbc.shape
    inner = z.shape[-1]
    L = min(SSD_L, s)
    G, N = SSD_GROUPS, SSD_STATE
    blk = lambda n: pl.BlockSpec((None, L, n), lambda i, j: (i, j, 0))
    return pl.pallas_call(
        _ssd_scan_kernel,
        grid=(b, s // L),
        in_specs=[blk(cc), blk(LANES), blk(inner), _const_spec(conv_w.shape), _const_spec((1, cc)),
                  _const_spec((1, LANES)), _const_spec((1, LANES)), _const_spec((1, inner)),
                  _const_spec((1, inner))],
        out_specs=blk(inner),
        out_shape=jax.ShapeDtypeStruct((b, s, inner), BF16),
        scratch_shapes=[pltpu.VMEM((CONV_HALO + L, cc), F32), pltpu.VMEM((L, cc), F32),
                        pltpu.VMEM((G, N, inner // G), F32), pltpu.VMEM((L, inner), F32)],
        compiler_params=_cparams("parallel", "arbitrary"),
        name="ssd_scan",
    )(xbc, dt, z, conv_w, conv_b, dt_bias, a_log, d_skip, norm_g)


def _pad_lanes(v):
    return jnp.zeros((1, LANES), F32).at[0, :v.shape[0]].set(v.astype(F32))


def _prep_ssd(w_in, conv_w, conv_b, dt_bias, a_log, d_skip, norm_g, w_out):
    d, n_in = w_in.shape
    n_heads = dt_bias.shape[0]
    inner = n_heads * SSD_HEADDIM
    cc = conv_w.shape[1]
    w = jnp.zeros((d, inner + cc + LANES), BF16).at[:, :n_in].set(w_in.astype(BF16))
    return (w, conv_w.astype(F32), conv_b.astype(F32)[None], _pad_lanes(dt_bias), _pad_lanes(a_log),
            jnp.repeat(d_skip.astype(F32), SSD_HEADDIM)[None], norm_g.astype(F32)[None], w_out.astype(BF16))


def _ssd_layer(h, g_pre, g_post, w, conv_w, conv_b, dt_bias, a_log, d_skip, norm_g, w_out):
    b, s, d = h.shape
    inner = norm_g.shape[-1]
    cc = conv_w.shape[1]
    hf = h.reshape(b * s, d)
    z, xbc, dt = _ssd_proj(hf, g_pre, w, inner, cc)
    y = _ssd_scan(xbc.reshape(b, s, cc), dt.reshape(b, s, LANES), z.reshape(b, s, inner),
                  conv_w, conv_b, dt_bias, a_log, d_skip, norm_g)
    return _outproj([y.reshape(b * s, inner)], [w_out], hf, g_post).reshape(b, s, d)


SB_HEADS = 8
NSA_HEADS = 8
NSA_GROUPS = 2
NSA_REP = NSA_HEADS // NSA_GROUPS
CMP_LEN = 32
CMP_STRIDE = 16
SEL_BLOCK = 64
SEL_TOPN = 16
WINDOW = 512
FORCE_BONUS = 1e4
NSA_QB = 128
SB_W = SB_HEADS * HEAD_DIM
NSA_QW = NSA_HEADS * HEAD_DIM
NSA_KVW = NSA_GROUPS * HEAD_DIM
LOG2E = 1.4426950408889634
ATT_SCALE = HEAD_DIM ** -0.5 * LOG2E

_C_QSB, _C_KSB, _C_VSB = 0, SB_W, 2 * SB_W
_C_QN = 3 * SB_W
_C_QNP = _C_QN + NSA_QW
_C_K3 = _C_QNP + NSA_QW
_C_K3P = _C_K3 + 3 * NSA_KVW
_C_V3 = _C_K3P + 3 * NSA_KVW
_C_GATE = _C_V3 + 3 * NSA_KVW
_C_END = _C_GATE + LANES


def _hyb_proj_kernel(h_ref, g_ref, pos_ref, inv_ref, sgn_ref, w_ref,
                     qsb_ref, ksb_ref, vsb_ref, qn_ref, kcmp_ref, vcmp_ref,
                     ksel_ref, vsel_ref, kwin_ref, vwin_ref, gate_ref):
    xn = _rms(h_ref[...], g_ref[...]).astype(BF16)
    tm = xn.shape[0]
    hd = HEAD_DIM

    def proj(lo, n):
        return _dot(xn, w_ref[:, lo:lo + n])

    for ref, lo, scale in ((qsb_ref, _C_QSB, ATT_SCALE), (ksb_ref, _C_KSB, 1.0), (vsb_ref, _C_VSB, 1.0)):
        p = proj(lo, SB_W) * scale
        for a in range(SB_HEADS):
            ref[a] = p[:, a * hd:(a + 1) * hd].astype(BF16)

    ang = pos_ref[...] * inv_ref[...]
    cos = jnp.cos(ang)
    sin = jnp.sin(ang) * sgn_ref[...]

    def rot(lo, lop, n):
        reps = n // LANES
        return (proj(lo, n) * jnp.concatenate([cos] * reps, axis=1)
                + proj(lop, n) * jnp.concatenate([sin] * reps, axis=1))

    qn = rot(_C_QN, _C_QNP, NSA_QW) * ATT_SCALE
    lane = lax.broadcasted_iota(jnp.int32, (tm, LANES), 1)
    low = lane < hd
    for a in range(NSA_HEADS):
        grp = a // NSA_REP
        piece = qn[:, (a // 2) * LANES:(a // 2 + 1) * LANES]
        if (a % 2) != grp:
            piece = pltpu.roll(piece, hd, 1)
        keep = low if grp == 0 else jnp.logical_not(low)
        qn_ref[a] = jnp.where(keep, piece, 0.0).astype(BF16)

    k3 = rot(_C_K3, _C_K3P, 3 * NSA_KVW)
    v3 = proj(_C_V3, 3 * NSA_KVW)
    for grp in range(NSA_GROUPS):
        kcmp_ref[grp] = k3[:, grp * hd:(grp + 1) * hd].astype(BF16)
        vcmp_ref[grp] = v3[:, grp * hd:(grp + 1) * hd].astype(BF16)
    ksel_ref[...] = k3[:, NSA_KVW:2 * NSA_KVW].astype(BF16)
    vsel_ref[...] = v3[:, NSA_KVW:2 * NSA_KVW].astype(BF16)
    kwin_ref[...] = k3[:, 2 * NSA_KVW:].astype(BF16)
    vwin_ref[...] = v3[:, 2 * NSA_KVW:].astype(BF16)
    gate_ref[...] = jax.nn.sigmoid(proj(_C_GATE, LANES))


def _hyb_proj(h, g, posf, inv, sgn, w):
    b, s, d = h.shape
    tm = min(TOKEN_TILE, s)
    heads = lambda n, w_: pl.BlockSpec((None, n, tm, w_), lambda i, j: (i, 0, j, 0))
    rows = lambda w_: pl.BlockSpec((None, tm, w_), lambda i, j: (i, j, 0))
    hshape = lambda n, w_: jax.ShapeDtypeStruct((b, n, s, w_), BF16)
    rshape = lambda w_, dt: jax.ShapeDtypeStruct((b, s, w_), dt)
    return pl.pallas_call(
        _hyb_proj_kernel,
        grid=(b, s // tm),
        in_specs=[rows(d), _const_spec((1, d)), rows(1), _const_spec((1, LANES)), _const_spec((1, LANES)),
                  _const_spec(w.shape)],
        out_specs=[heads(SB_HEADS, HEAD_DIM)] * 3 + [heads(NSA_HEADS, LANES)]
        + [heads(NSA_GROUPS, HEAD_DIM)] * 2 + [rows(LANES)] * 5,
        out_shape=[hshape(SB_HEADS, HEAD_DIM)] * 3 + [hshape(NSA_HEADS, LANES)]
        + [hshape(NSA_GROUPS, HEAD_DIM)] * 2 + [rshape(LANES, BF16)] * 4 + [rshape(LANES, F32)],
        compiler_params=_cparams("parallel", "parallel"),
        name="hyb_proj",
    )(h, g, posf, inv, sgn, w)


SB_TILE = 256
SB_SUB = 128
SB_DEAD_LOG2 = 160.0


def _sb_blocks(ops, later2, mask):
    ts = [_dot_nt(q, k) for q, k, _, _ in ops]
    sps, hls = [], []
    for t in ts:
        neg_abs = lax.bitcast_convert_type(lax.bitcast_convert_type(t, jnp.uint32) | jnp.uint32(0x80000000),
                                           F32)
        sp = jnp.maximum(t, 0.0) + jnp.log2(1.0 + jnp.exp2(neg_abs))
        if mask is not None:
            sp = jnp.where(mask, sp, 0.0)
        hi = sp.astype(BF16)
        lo = (sp - hi.astype(F32)).astype(BF16)
        sps.append(sp)
        hls.append(jnp.concatenate([hi, lo], axis=1))
    css = [_dot(hl, later2) for hl in hls]
    ws = []
    for t, sp, cs, (_, _, _, carry) in zip(ts, sps, css, ops):
        w = jnp.exp2(t - sp - cs - carry)
        if mask is not None:
            w = jnp.where(mask, w, 0.0)
        ws.append(w.astype(BF16))
    return [(_dot(w, v), carry + cs[:, 0:1] + sp[:, 0:1])
            for w, sp, cs, (_, _, v, carry) in zip(ws, sps, css, ops)]


def _sb_kernel(q_ref, kd_ref, vd_ref, k_hbm, v_hbm, o_ref, acc_ref, carry_ref, kbuf, vbuf, sem):
    bi = pl.program_id(0)
    i = pl.program_id(1)
    n_heads, tq, hd = q_ref.shape
    sub = kbuf.shape[2]
    n_diag = tq // sub
    n_prev = i * n_diag

    def copies(n, slot):
        start = pl.multiple_of((n_prev - 1 - n) * sub, sub)
        return (pltpu.make_async_copy(k_hbm.at[bi, :, pl.ds(start, sub), :], kbuf.at[slot], sem.at[0, slot]),
                pltpu.make_async_copy(v_hbm.at[bi, :, pl.ds(start, sub), :], vbuf.at[slot], sem.at[1, slot]))

    def start_fetch(n, slot):
        for c in copies(n, slot):
            c.start()

    def wait_fetch(n, slot):
        for c in copies(n, slot):
            c.wait()

    @pl.when(n_prev > 0)
    def _():
        start_fetch(0, 0)

    kr = lax.broadcasted_iota(jnp.int32, (2 * sub, sub), 0) % sub
    kc = lax.broadcasted_iota(jnp.int32, (2 * sub, sub), 1)
    later2 = jnp.where(kr > kc, 1.0, 0.0).astype(BF16)

    acc_ref[...] = jnp.zeros(acc_ref.shape, F32)
    carry_ref[...] = jnp.zeros(carry_ref.shape, F32)

    for sb in reversed(range(n_diag)):
        r0 = sb * sub
        rows = tq - r0
        mask = (lax.broadcasted_iota(jnp.int32, (rows, sub), 1)
                < lax.broadcasted_iota(jnp.int32, (rows, sub), 0))
        ops = [(q_ref[a, r0:, :], kd_ref[a, r0:r0 + sub, :], vd_ref[a, r0:r0 + sub, :], carry_ref[a, r0:, :])
               for a in range(n_heads)]
        for a, (pv, carry) in enumerate(_sb_blocks(ops, later2, mask)):
            acc_ref[a, r0:, :] += pv
            carry_ref[a, r0:, :] = carry

    def live():
        return jnp.min(carry_ref[...]) < SB_DEAD_LOG2

    def body(state):
        n, _ = state
        slot = n % 2
        wait_fetch(n, slot)

        @pl.when(n + 1 < n_prev)
        def _():
            start_fetch(n + 1, 1 - slot)

        ops = [(q_ref[a], kbuf[slot, a], vbuf[slot, a], carry_ref[a]) for a in range(n_heads)]
        for a, (pv, carry) in enumerate(_sb_blocks(ops, later2, None)):
            acc_ref[a] += pv
            carry_ref[a] = carry
        return n + 1, live()

    n_end, _ = lax.while_loop(lambda st: (st[0] < n_prev) & st[1], body, (jnp.int32(0), live()))

    @pl.when(n_end < n_prev)
    def _():
        wait_fetch(n_end, n_end % 2)

    for a in range(n_heads):
        o_ref[:, a * hd:(a + 1) * hd] = acc_ref[a].astype(o_ref.dtype)


def _sb_attn(q, k, v):
    b, n_heads, s, hd = q.shape
    t = min(SB_TILE, s)
    sub = min(SB_SUB, t)
    tile = pl.BlockSpec((None, n_heads, t, hd), lambda bi, i: (bi, 0, i, 0))
    hbm = pl.BlockSpec(memory_space=pl.ANY)
    return pl.pallas_call(
        _sb_kernel,
        grid=(b, s // t),
        in_specs=[tile, tile, tile, hbm, hbm],
        out_specs=pl.BlockSpec((None, t, n_heads * hd), lambda bi, i: (bi, i, 0)),
        out_shape=jax.ShapeDtypeStruct((b, s, n_heads * hd), BF16),
        scratch_shapes=[pltpu.VMEM((n_heads, t, hd), F32), pltpu.VMEM((n_heads, t, 1), F32),
                        pltpu.VMEM((2, n_heads, sub, hd), BF16), pltpu.VMEM((2, n_heads, sub, hd), BF16),
                        pltpu.SemaphoreType.DMA((2, 2))],
        compiler_params=_cparams("parallel", "arbitrary"),
        name="sb_attn",
    )(q, k, v, k, v)


def _compress_kernel(t_ref, wab_ref, pe_ref, w1_ref, w2_ref, o_ref):
    n_grp, nch, _ = t_ref.shape
    hid = w2_ref.shape[0]
    pe8 = jnp.broadcast_to(pe_ref[...], (8, pe_ref.shape[1])).astype(BF16)
    bias = _dot(pe8, w1_ref[...])[0:1, :]
    for grp in range(n_grp):
        ab = _dot(t_ref[grp], wab_ref[...])
        nxt = pltpu.roll(ab[:, hid:], nch - 1, 0)
        hcur = ab[:, :hid] + nxt + bias
        act = (hcur * jax.nn.sigmoid(hcur)).astype(BF16)
        o_ref[:, grp * HEAD_DIM:(grp + 1) * HEAD_DIM] = _dot(act, w2_ref[...]).astype(o_ref.dtype)


def _compress(t, wab, pe, w1, w2):
    b, n_grp, nch, kw = t.shape
    return pl.pallas_call(
        _compress_kernel,
        grid=(b,),
        in_specs=[pl.BlockSpec((None, n_grp, nch, kw), lambda i: (i, 0, 0, 0)), _const_spec(wab.shape),
                  _const_spec(pe.shape), _const_spec(w1.shape), _const_spec(w2.shape)],
        out_specs=pl.BlockSpec((None, nch, n_grp * HEAD_DIM), lambda i: (i, 0, 0)),
        out_shape=jax.ShapeDtypeStruct((b, nch, n_grp * HEAD_DIM), BF16),
        compiler_params=_cparams("parallel"),
        name="nsa_compress",
    )(t, wab, pe, w1, w2)


def _masked_softmax_rows(s, mask):
    sm = jnp.where(mask, s, NEG_BIG)
    m = jnp.max(sm, axis=-1, keepdims=True)
    e = jnp.where(mask, jnp.exp2(sm - m), 0.0)
    den = jnp.sum(e, axis=-1, keepdims=True)
    return e / jnp.maximum(den, 1e-30)


def _softmax_num(s, bias, groups):
    rows, n = s.shape
    sm = (s.reshape(groups, rows // groups, n) + bias[None]).reshape(rows, n)
    e = jnp.exp2(sm - jnp.max(sm, axis=-1, keepdims=True))
    return e.astype(BF16), jnp.sum(e, axis=-1, keepdims=True)


def _select_kernel(qf_ref, kc_ref, o_ref):
    n_heads, nsub, _ = qf_ref.shape
    nch = kc_ref.shape[0]
    q = qf_ref[...].reshape(n_heads * nsub, LANES)
    s = _dot_nt(q, kc_ref[...])
    sub_id = lax.broadcasted_iota(jnp.int32, (n_heads * nsub, nch), 0) % nsub
    c_id = lax.broadcasted_iota(jnp.int32, (n_heads * nsub, nch), 1)
    p = _masked_softmax_rows(s, c_id * CMP_STRIDE + (CMP_LEN - 1) <= sub_id * SEL_BLOCK)
    rows = NSA_GROUPS * nsub
    psum = jnp.concatenate(
        [sum(p[(grp * NSA_REP + r) * nsub:(grp * NSA_REP + r + 1) * nsub] for r in range(NSA_REP))
         for grp in range(NSA_GROUPS)], axis=0)
    oc = lax.broadcasted_iota(jnp.int32, (nch, LANES), 0) * CMP_STRIDE
    oj = lax.broadcasted_iota(jnp.int32, (nch, LANES), 1) * SEL_BLOCK
    overlap = jnp.where((oc < oj + SEL_BLOCK) & (oc + (CMP_LEN - 1) >= oj), 1.0, 0.0).astype(BF16)
    p1, p2, p3 = _split3(psum)
    imp = _dot(p1, overlap) + _dot(p2, overlap) + _dot(p3, overlap)
    jl = lax.broadcasted_iota(jnp.int32, (rows, LANES), 1)
    cur = lax.broadcasted_iota(jnp.int32, (rows, LANES), 0) % nsub
    forced = (jl == 0) | (jl == cur) | (jl == cur - 1)
    score = jnp.where(jl <= cur, imp + jnp.where(forced, FORCE_BONUS, 0.0), -1.0)

    def body(d, rank):
        other = pltpu.roll(score, d, 1)
        ahead = (other > score) | ((other == score) & (jl >= d))
        return rank + jnp.where(ahead, 1.0, 0.0)

    rank = lax.fori_loop(1, LANES, body, jnp.zeros((rows, LANES), F32))
    jf = jl.astype(F32)
    out = jnp.zeros((rows, LANES), F32)
    for r in range(SEL_TOPN):
        col = jnp.sum(jnp.where(rank == float(r), jf, 0.0), axis=-1, keepdims=True)
        out = jnp.where(jl == r, col, out)
    o_ref[...] = out.astype(jnp.int32)


def _nsa_select(qf, kc):
    b, n_heads, nsub, _ = qf.shape
    nch = kc.shape[1]
    assert SEL_TOPN <= nsub <= LANES
    return pl.pallas_call(
        _select_kernel,
        grid=(b,),
        in_specs=[pl.BlockSpec((None, n_heads, nsub, LANES), lambda i: (i, 0, 0, 0)),
                  pl.BlockSpec((None, nch, LANES), lambda i: (i, 0, 0))],
        out_specs=pl.BlockSpec((None, NSA_GROUPS * nsub, LANES), lambda i: (i, 0, 0)),
        out_shape=jax.ShapeDtypeStruct((b, NSA_GROUPS * nsub, LANES), jnp.int32),
        compiler_params=_cparams("parallel"),
        name="nsa_select",
    )(qf, kc)


def _nsa_kernel(sel_ref, qn_ref, kc_ref, vc_ref, ksel_ref, vsel_ref, kwin_ref, vwin_ref, gate_ref,
                o_ref, kg_ref, vg_ref):
    bi = pl.program_id(0)
    i = pl.program_id(1)
    n_heads, qb, _ = qn_ref.shape
    s_len = ksel_ref.shape[0]
    nsub_total = s_len // SEL_BLOCK
    nch = kc_ref.shape[0]
    hd = HEAD_DIM
    q = qn_ref[...].reshape(n_heads * qb, LANES)
    qpos = i * qb + lax.broadcasted_iota(jnp.int32, (qb, 1), 0)

    def bias_of(mask):
        return jnp.where(mask, 0.0, NEG_BIG)

    nsel = SEL_TOPN * SEL_BLOCK
    lane_blk = lax.broadcasted_iota(jnp.int32, (1, nsel), 1) // SEL_BLOCK
    lane_off = lax.broadcasted_iota(jnp.int32, (1, nsel), 1) % SEL_BLOCK
    n_sub = qb // SEL_BLOCK
    sel_jobs = []
    for grp in range(NSA_GROUPS):
        for n in range(n_sub):
            slot = grp * n_sub + n
            base = ((bi * NSA_GROUPS + grp) * nsub_total + i * n_sub + n) * SEL_TOPN
            tok = lane_off
            for t in range(SEL_TOPN):
                idx = sel_ref[base + t]
                off = pl.multiple_of(idx * SEL_BLOCK, SEL_BLOCK)
                kg_ref[slot, t * SEL_BLOCK:(t + 1) * SEL_BLOCK, :] = ksel_ref[pl.ds(off, SEL_BLOCK), :]
                vg_ref[slot, t * SEL_BLOCK:(t + 1) * SEL_BLOCK, :] = vsel_ref[pl.ds(off, SEL_BLOCK), :]
                tok = tok + jnp.where(lane_blk == t, idx * SEL_BLOCK, 0)
            qs = qn_ref[grp * NSA_REP:(grp + 1) * NSA_REP, n * SEL_BLOCK:(n + 1) * SEL_BLOCK, :]
            qp = i * qb + n * SEL_BLOCK + lax.broadcasted_iota(jnp.int32, (SEL_BLOCK, 1), 0)
            sel_jobs.append((slot, qs.reshape(NSA_REP * SEL_BLOCK, LANES), bias_of(tok <= qp)))

    c_end = lax.broadcasted_iota(jnp.int32, (1, nch), 1) * CMP_STRIDE + (CMP_LEN - 1)
    wlen = WINDOW + qb
    start = pl.multiple_of(jnp.maximum(i * qb - WINDOW, 0), qb)
    delta = qpos - (start + lax.broadcasted_iota(jnp.int32, (1, wlen), 1))

    s_c = _dot_nt(q, kc_ref[...])
    s_w = _dot_nt(q, kwin_ref[pl.ds(start, wlen), :])
    s_s = [_dot_nt(qs, kg_ref[slot]) for slot, qs, _ in sel_jobs]

    e_c, den_c = _softmax_num(s_c, bias_of(c_end <= qpos), n_heads)
    e_w, den_w = _softmax_num(s_w, bias_of((delta >= 0) & (delta < WINDOW)), n_heads)
    num_s = [_softmax_num(s, bias, NSA_REP) for s, (_, _, bias) in zip(s_s, sel_jobs)]

    any_c = jnp.where(qpos >= CMP_LEN - 1, 1.0, 0.0)
    o_c = _dot(e_c, vc_ref[...]) / den_c
    o_c = (o_c.reshape(n_heads, qb, LANES) * any_c[None]).reshape(n_heads * qb, LANES)
    o_w = _dot(e_w, vwin_ref[pl.ds(start, wlen), :]) / den_w
    o_sel = [_dot(e, vg_ref[slot]) / den for (e, den), (slot, _, _) in zip(num_s, sel_jobs)]
    o_s = [[o_sel[grp * n_sub + n] for n in range(n_sub)] for grp in range(NSA_GROUPS)]


    lane = lax.broadcasted_iota(jnp.int32, (qb, LANES), 1)
    low = lane < hd
    gates = gate_ref[...]
    mixed = []
    for a in range(n_heads):
        grp, r = a // NSA_REP, a % NSA_REP
        rs = slice(a * qb, (a + 1) * qb)
        sel_rows = jnp.concatenate([o_s[grp][n][r * SEL_BLOCK:(r + 1) * SEL_BLOCK] for n in range(n_sub)],
                                   axis=0)
        mixed.append(gates[:, 3 * a:3 * a + 1] * o_c[rs] + gates[:, 3 * a + 1:3 * a + 2] * sel_rows
                     + gates[:, 3 * a + 2:3 * a + 3] * o_w[rs])
    for pr in range(n_heads // 2):
        grp = (2 * pr) // NSA_REP
        left, right = mixed[2 * pr], mixed[2 * pr + 1]
        if grp == 0:
            right = pltpu.roll(right, hd, 1)
        else:
            left = pltpu.roll(left, hd, 1)
        o_ref[:, pr * LANES:(pr + 1) * LANES] = jnp.where(low, left, right).astype(o_ref.dtype)


def _nsa_attn(sel, qn, kc, vc, ksel, vsel, kwin, vwin, gates):
    b, n_heads, s, _ = qn.shape
    nch = kc.shape[1]
    qb = min(NSA_QB, s)
    assert s >= WINDOW + qb
    full = lambda n: pl.BlockSpec((None, n, LANES), lambda bi, i, sel: (bi, 0, 0))
    grid_spec = pltpu.PrefetchScalarGridSpec(
        num_scalar_prefetch=1,
        grid=(b, s // qb),
        in_specs=[pl.BlockSpec((None, n_heads, qb, LANES), lambda bi, i, sel: (bi, 0, i, 0)),
                  full(nch), full(nch), full(s), full(s), full(s), full(s),
                  pl.BlockSpec((None, qb, LANES), lambda bi, i, sel: (bi, i, 0))],
        out_specs=pl.BlockSpec((None, qb, n_heads * HEAD_DIM), lambda bi, i, sel: (bi, i, 0)),
        scratch_shapes=[pltpu.VMEM((NSA_GROUPS * (qb // SEL_BLOCK), SEL_TOPN * SEL_BLOCK, LANES), BF16)] * 2,
    )
    return pl.pallas_call(
        _nsa_kernel,
        grid_spec=grid_spec,
        out_shape=jax.ShapeDtypeStruct((b, s, n_heads * HEAD_DIM), BF16),
        compiler_params=_cparams("parallel", "arbitrary"),
        name="nsa_attn",
    )(sel, qn, kc, vc, ksel, vsel, kwin, vwin, gates)


def _prep_hybrid(w_in, w_out, pe_k, w1_k, w2_k, pe_v, w1_v, w2_v):
    d = w_in.shape[0]
    hd = HEAD_DIM
    col = np.arange(w_in.shape[1])
    o_qn = 3 * SB_W
    o_kv = o_qn + NSA_QW
    kv = lambda n: col[o_kv + n * NSA_KVW:o_kv + (n + 1) * NSA_KVW]
    partner = lambda c: (c // hd) * hd + (c % hd + hd // 2) % hd
    qn_c = col[o_qn:o_qn + NSA_QW]
    k3_c = np.concatenate([kv(0), kv(2), kv(4)])
    v3_c = np.concatenate([kv(1), kv(3), kv(5)])
    gate_c = col[o_kv + 6 * NSA_KVW:]
    order = np.concatenate([col[:o_qn], qn_c, partner(qn_c), k3_c, partner(k3_c), v3_c, gate_c])
    w = jnp.zeros((d, _C_END), BF16).at[:, :order.shape[0]].set(w_in[:, order].astype(BF16))
    lanes = np.arange(LANES)
    inv = (ROPE_THETA ** (-(lanes % ROPE_HALF).astype(np.float32) / ROPE_HALF)).astype(np.float32)[None]
    sgn = np.where(lanes % hd < ROPE_HALF, -1.0, 1.0).astype(np.float32)[None]

    def cmp_w(pe, w1, w2):
        half = w1.shape[0] // 2
        wab = jnp.concatenate([w1[:half], w1[half:]], axis=1).astype(BF16)
        return wab, pe.reshape(1, -1).astype(F32), w1.astype(BF16), w2.astype(BF16)

    return (w, jnp.asarray(inv), jnp.asarray(sgn), w_out[:SB_W].astype(BF16), w_out[SB_W:].astype(BF16),
            cmp_w(pe_k, w1_k, w2_k), cmp_w(pe_v, w1_v, w2_v))


def _hybrid_layer(h, positions, g_pre, g_post, w, inv, sgn, wo_sb, wo_nsa, cmp_k, cmp_v):
    b, s, d = h.shape
    posf = positions.astype(F32)[..., None]
    (qsb, ksb, vsb, qn, kcmp, vcmp, ksel, vsel, kwin, vwin, gates) = _hyb_proj(h, g_pre, posf, inv, sgn, w)
    o_sb = _sb_attn(qsb, ksb, vsb)
    half = CMP_LEN // 2
    chunks = lambda t: t.reshape(b, NSA_GROUPS, s // half, half * HEAD_DIM)
    kc = _compress(chunks(kcmp), *cmp_k)
    vc = _compress(chunks(vcmp), *cmp_v)
    sel = _nsa_select(qn[:, :, ::SEL_BLOCK, :], kc)
    sel = sel[:, :, :SEL_TOPN].reshape(-1)
    o_nsa = _nsa_attn(sel, qn, kc, vc, ksel, vsel, kwin, vwin, gates)
    hf = h.reshape(b * s, d)
    return _outproj([o_sb.reshape(b * s, SB_W), o_nsa.reshape(b * s, NSA_QW)], [wo_sb, wo_nsa], hf,
                    g_post).reshape(b, s, d)


CROSS_HEADS = 4


def kernel(x, mem, positions, norm_g, ffn1_w_gu, ffn1_w_down, ffn2_w_gu, ffn2_w_down, cross_wq, cross_wkv,
           cross_wo, hyb_w_in, hyb_w_out, cmp_pe_k, cmp_w1_k, cmp_w2_k, cmp_pe_v, cmp_w1_v, cmp_w2_v,
           ssd_w_in, ssd_conv_w, ssd_conv_b, ssd_dt_bias, ssd_A_log, ssd_D, ssd_norm_g, ssd_w_out):
    b, s, d = x.shape
    depth = norm_g.shape[0]
    h = x
    for i in range(depth):
        g = norm_g[i].astype(F32)[:, None, :]
        j = i // 2
        h = _ffn(h.reshape(b * s, d), g[0], g[1], ffn1_w_gu[i].astype(BF16),
                 ffn1_w_down[i].astype(BF16)).reshape(b, s, d)
        if i % 2 == 0:
            h = _hybrid_layer(h, positions, g[2], g[3],
                              *_prep_hybrid(hyb_w_in[j], hyb_w_out[j], cmp_pe_k[j], cmp_w1_k[j], cmp_w2_k[j],
                                            cmp_pe_v[j], cmp_w1_v[j], cmp_w2_v[j]))
        else:
            h = _ssd_layer(h, g[2], g[3], *_prep_ssd(ssd_w_in[j], ssd_conv_w[j], ssd_conv_b[j], ssd_dt_bias[j],
                                                     ssd_A_log[j], ssd_D[j], ssd_norm_g[j], ssd_w_out[j]))
        k, v = _memkv(mem, g[6], cross_wkv[i].astype(BF16))
        h = _cross(h, g[4], g[5], cross_wq[i].astype(BF16), k, v, cross_wo[i].astype(BF16), CROSS_HEADS)
        h = _ffn(h.reshape(b * s, d), g[7], g[8], ffn2_w_gu[i].astype(BF16),
                 ffn2_w_down[i].astype(BF16)).reshape(b, s, d)
    return h
```

```python
import functools
import math

import jax
import jax.numpy as jnp
import numpy as np
from jax import lax
from jax.experimental import pallas as pl
from jax.experimental.pallas import tpu as pltpu

F32 = jnp.float32
BF16 = jnp.bfloat16

RMS_EPS = 1e-6
LOG2E = 1.4426950408889634
ROPE_THETA = 10000.0
HEAD_DIM = 64
ROPE_HALF = HEAD_DIM // 2

V7X_VMEM_BYTES = 64 * 1024 * 1024
VMEM_LIMIT = (V7X_VMEM_BYTES * 3) // 4
LANES = 128

TOKEN_TILE = 512
FF_CHUNK = 512


def _cparams(*sem):
    return pltpu.CompilerParams(dimension_semantics=sem, vmem_limit_bytes=VMEM_LIMIT)


def _rms(x, g):
    ms = jnp.mean(x * x, axis=-1, keepdims=True)
    return x * lax.rsqrt(ms + RMS_EPS) * g


def _dot(a, b):
    return jnp.dot(a, b, preferred_element_type=F32)


def _dot_nt(a, b):
    return lax.dot_general(a, b, (((1,), (1,)), ((), ())), preferred_element_type=F32)


def _const_spec(shape):
    nd = len(shape)
    return pl.BlockSpec(shape, lambda *_: (0,) * nd)


def _ffn_kernel(h_ref, gpre_ref, gpost_ref, wgu_ref, wd_ref, o_ref):
    d_ff = wd_ref.shape[0]
    h = h_ref[...]
    xn = _rms(h, gpre_ref[...]).astype(BF16)
    acc = jnp.zeros(h.shape, F32)
    for c in range(d_ff // FF_CHUNK):
        lo = c * FF_CHUNK
        gate = _dot(xn, wgu_ref[:, lo:lo + FF_CHUNK])
        up = _dot(xn, wgu_ref[:, d_ff + lo:d_ff + lo + FF_CHUNK])
        act = (gate * jax.nn.sigmoid(gate) * up).astype(BF16)
        acc = acc + _dot(act, wd_ref[lo:lo + FF_CHUNK, :])
    o_ref[...] = h + 0.5 * _rms(acc, gpost_ref[...])


def _ffn(h, g_pre, g_post, w_gu, w_down):
    t, d = h.shape
    d_ff = w_down.shape[0]
    tm = min(TOKEN_TILE, t)
    return pl.pallas_call(
        _ffn_kernel,
        grid=(t // tm,),
        in_specs=[
            pl.BlockSpec((tm, d), lambda i: (i, 0)),
            _const_spec((1, d)),
            _const_spec((1, d)),
            _const_spec((d, 2 * d_ff)),
            _const_spec((d_ff, d)),
        ],
        out_specs=pl.BlockSpec((tm, d), lambda i: (i, 0)),
        out_shape=jax.ShapeDtypeStruct((t, d), F32),
        compiler_params=_cparams("parallel"),
        name="ffn",
    )(h, g_pre, g_post, w_gu, w_down)


def _memkv_kernel(mem_ref, g_ref, wkv_ref, k_ref, v_ref):
    d = mem_ref.shape[-1]
    mn = _rms(mem_ref[...], g_ref[...]).astype(BF16)
    kv = _dot(mn, wkv_ref[...])
    k_ref[...] = kv[:, :d].astype(BF16)
    v_ref[...] = kv[:, d:].astype(BF16)


def _memkv(mem, g_mem, wkv):
    b, m, d = mem.shape
    return pl.pallas_call(
        _memkv_kernel,
        grid=(b,),
        in_specs=[
            pl.BlockSpec((None, m, d), lambda i: (i, 0, 0)),
            _const_spec((1, d)),
            _const_spec((d, 2 * d)),
        ],
        out_specs=[pl.BlockSpec((None, m, d), lambda i: (i, 0, 0))] * 2,
        out_shape=[jax.ShapeDtypeStruct((b, m, d), BF16)] * 2,
        compiler_params=_cparams("parallel"),
        name="memkv",
    )(mem, g_mem, wkv)


def _cross_kernel(n_heads, h_ref, gpre_ref, gpost_ref, wq_ref, k_ref, v_ref, wo_ref, o_ref):
    h = h_ref[...]
    d = h.shape[-1]
    hd = d // n_heads
    hn = _rms(h, gpre_ref[...]).astype(BF16)
    q = (_dot(hn, wq_ref[...]) * (hd ** -0.5 * LOG2E)).astype(BF16)
    sls = [slice(a * hd, (a + 1) * hd) for a in range(n_heads)]
    ss = [_dot_nt(q[:, sl], k_ref[:, sl]) for sl in sls]
    es = [jnp.exp2(s - jnp.max(s, axis=-1, keepdims=True)) for s in ss]
    os_ = [(_dot(e.astype(BF16), v_ref[:, sl]) / jnp.sum(e, axis=-1, keepdims=True)).astype(BF16)
           for e, sl in zip(es, sls)]
    c = _dot(jnp.concatenate(os_, axis=1), wo_ref[...])
    o_ref[...] = h + _rms(c, gpost_ref[...])


def _cross(h, g_pre, g_post, wq, k, v, wo, n_heads):
    b, s, d = h.shape
    m = k.shape[1]
    tm = min(TOKEN_TILE, s)
    return pl.pallas_call(
        functools.partial(_cross_kernel, n_heads),
        grid=(b, s // tm),
        in_specs=[
            pl.BlockSpec((None, tm, d), lambda i, j: (i, j, 0)),
            _const_spec((1, d)),
            _const_spec((1, d)),
            _const_spec((d, d)),
            pl.BlockSpec((None, m, d), lambda i, j: (i, 0, 0)),
            pl.BlockSpec((None, m, d), lambda i, j: (i, 0, 0)),
            _const_spec((d, d)),
        ],
        out_specs=pl.BlockSpec((None, tm, d), lambda i, j: (i, j, 0)),
        out_shape=jax.ShapeDtypeStruct((b, s, d), F32),
        compiler_params=_cparams("parallel", "parallel"),
        name="cross",
    )(h, g_pre, g_post, wq, k, v, wo)


def _outproj_kernel(n_in, *refs):
    a_refs = refs[:n_in]
    w_refs = refs[n_in:2 * n_in]
    h_ref, g_ref, o_ref = refs[2 * n_in:]
    m = _dot(a_refs[0][...], w_refs[0][...])
    for a_ref, w_ref in zip(a_refs[1:], w_refs[1:]):
        m = m + _dot(a_ref[...], w_ref[...])
    o_ref[...] = h_ref[...] + _rms(m, g_ref[...])


def _outproj(acts, weights, h, g):
    t, d = h.shape
    tm = min(TOKEN_TILE, t)
    n_in = len(acts)
    return pl.pallas_call(
        functools.partial(_outproj_kernel, n_in),
        grid=(t // tm,),
        in_specs=(
            [pl.BlockSpec((tm, a.shape[1]), lambda i: (i, 0)) for a in acts]
            + [_const_spec(w.shape) for w in weights]
            + [pl.BlockSpec((tm, d), lambda i: (i, 0)), _const_spec((1, d))]
        ),
        out_specs=pl.BlockSpec((tm, d), lambda i: (i, 0)),
        out_shape=jax.ShapeDtypeStruct((t, d), F32),
        compiler_params=_cparams("parallel"),
        name="outproj",
    )(*acts, *weights, h, g)


SSD_HEADDIM = 64
SSD_GROUPS = 4
SSD_STATE = 128
SSD_CONV = 4
SSD_L = 128
CONV_HALO = 16
NEG_BIG = -1e30


def _split3(x):
    a = x.astype(BF16)
    r = x - a.astype(F32)
    b = r.astype(BF16)
    c = (r - b.astype(F32)).astype(BF16)
    return a, b, c


def _softplus(x):
    return jnp.maximum(x, 0.0) + jnp.log(1.0 + jnp.exp(-jnp.abs(x)))


def _ssd_proj_kernel(h_ref, halo_ref, g_ref, w_ref, cw_ref, cb_ref, z_ref, xc_ref, dt_ref):
    tm = h_ref.shape[0]
    inner = z_ref.shape[-1]
    cc = xc_ref.shape[-1]
    H0 = CONV_HALO
    n_sh = SSD_CONV - 1
    xn = _rms(h_ref[...], g_ref[...]).astype(BF16)
    halo = jnp.where(pl.program_id(1) > 0, _rms(halo_ref[...], g_ref[...]), 0.0).astype(BF16)
    xe = jnp.concatenate([halo, xn], axis=0)
    z_chunks = list(range(0, inner, FF_CHUNK))
    for n, lo in enumerate(range(0, cc, FF_CHUNK)):
        sl = slice(lo, lo + FF_CHUNK)
        p = _dot(xe, w_ref[:, inner + lo:inner + lo + FF_CHUNK])
        if n < len(z_chunks):
            zl = z_chunks[n]
            z_ref[:, zl:zl + FF_CHUNK] = _dot(xn, w_ref[:, zl:zl + FF_CHUNK]).astype(BF16)
        acc = cb_ref[:, sl] + cw_ref[n_sh:n_sh + 1, sl] * p[H0:, :]
        for k in range(n_sh):
            acc = acc + cw_ref[k:k + 1, sl] * p[H0 - n_sh + k:H0 - n_sh + k + tm, :]
        xc_ref[:, sl] = (acc * jax.nn.sigmoid(acc)).astype(BF16)
    for zl in z_chunks[cc // FF_CHUNK:]:
        z_ref[:, zl:zl + FF_CHUNK] = _dot(xn, w_ref[:, zl:zl + FF_CHUNK]).astype(BF16)
    dt_ref[...] = _dot(xn, w_ref[:, inner + cc:])


def _ssd_proj(h, g, w, conv_w, conv_b, inner, cc):
    b, s, d = h.shape
    tm = min(TOKEN_TILE, s)
    per_halo = tm // CONV_HALO
    row = lambda n: pl.BlockSpec((None, tm, n), lambda i, j: (i, j, 0))
    return pl.pallas_call(
        _ssd_proj_kernel,
        grid=(b, s // tm),
        in_specs=[row(d),
                  pl.BlockSpec((None, CONV_HALO, d), lambda i, j: (i, jnp.maximum(j * per_halo - 1, 0), 0)),
                  _const_spec((1, d)), _const_spec(w.shape), _const_spec(conv_w.shape), _const_spec((1, cc))],
        out_specs=[row(inner), row(cc), row(LANES)],
        out_shape=[jax.ShapeDtypeStruct((b, s, inner), BF16), jax.ShapeDtypeStruct((b, s, cc), BF16),
                   jax.ShapeDtypeStruct((b, s, LANES), F32)],
        compiler_params=_cparams("parallel", "arbitrary"),
        name="ssd_proj",
    )(h, h, g, w, conv_w, conv_b)


def _ssd_scan_kernel(xc_ref, dt_ref, dtb_ref, alog_ref, dskip_ref, y_ref, state_ref):
    L = xc_ref.shape[0]
    inner = y_ref.shape[1]
    G, N, P = SSD_GROUPS, SSD_STATE, SSD_HEADDIM
    hpg = inner // P // G

    @pl.when(pl.program_id(1) == 0)
    def _():
        state_ref[...] = jnp.zeros(state_ref.shape, F32)

    dt = _softplus(dt_ref[...] + dtb_ref[...])
    da = dt * (jnp.exp(alog_ref[...]) * -LOG2E)
    ti = lax.broadcasted_iota(jnp.int32, (L, L), 0)
    si = lax.broadcasted_iota(jnp.int32, (L, L), 1)
    causal = si <= ti
    tri = jnp.where(causal, 1.0, 0.0).astype(BF16)
    d1, d2, d3 = _split3(da)
    acum = _dot(tri, d1) + _dot(tri, d2) + _dot(tri, d3)
    acum_t = acum.T
    dt_t = dt.T
    dec_end_t = jnp.exp2(acum_t[:, L - 1:L] - acum_t) * dt_t
    e_last = jnp.exp2(acum[L - 1:L, :])
    lane = lax.broadcasted_iota(jnp.int32, (L, 2 * P), 1)
    first = lane < P
    lane1 = lax.broadcasted_iota(jnp.int32, (1, 2 * P), 1)
    first1 = lane1 < P

    for g in range(G):
        bg = xc_ref[:, inner + g * N:inner + (g + 1) * N]
        cg_b = xc_ref[:, inner + G * N + g * N:inner + G * N + (g + 1) * N]
        cg = cg_b.astype(F32)
        cbm = _dot_nt(cg_b, bg)
        bg_t = bg.astype(F32).T
        for pr in range(hpg // 2):
            h0 = g * hpg + 2 * pr
            xsl = slice(h0 * P, (h0 + 2) * P)
            psl = slice(2 * pr * P, (2 * pr + 2) * P)
            x_b = xc_ref[:, xsl]
            st_pair = state_ref[g, :, psl]
            rhs = jnp.concatenate([x_b, st_pair.astype(BF16)], axis=0)
            ys, us = [], []
            for h in (h0, h0 + 1):
                colb = jnp.broadcast_to(acum[:, h:h + 1], (L, L))
                decay = jnp.exp2(jnp.where(causal, colb - acum_t[h:h + 1, :], NEG_BIG))
                w = cbm * decay * dt_t[h:h + 1, :]
                cs = cg * jnp.exp2(colb)
                lhs = jnp.concatenate([w, cs], axis=1).astype(BF16)
                ys.append(_dot(lhs, rhs))
                us.append(_dot((bg_t * dec_end_t[h:h + 1, :]).astype(BF16), x_b))
            y_ref[:, xsl] = (jnp.where(first, ys[0], ys[1])
                             + dskip_ref[:, xsl] * x_b.astype(F32)).astype(y_ref.dtype)
            sd = jnp.where(first1, e_last[:, h0:h0 + 1], e_last[:, h0 + 1:h0 + 2])
            state_ref[g, :, psl] = st_pair * sd + jnp.where(first, us[0], us[1])


def _ssd_scan(xc, dt, dt_bias, a_log, d_skip, inner):
    b, s, cc = xc.shape
    L = min(SSD_L, s)
    G, N = SSD_GROUPS, SSD_STATE
    blk = lambda n: pl.BlockSpec((None, L, n), lambda i, j: (i, j, 0))
    return pl.pallas_call(
        _ssd_scan_kernel,
        grid=(b, s // L),
        in_specs=[blk(cc), blk(LANES), _const_spec((1, LANES)), _const_spec((1, LANES)),
                  _const_spec((1, inner))],
        out_specs=blk(inner),
        out_shape=jax.ShapeDtypeStruct((b, s, inner), BF16),
        scratch_shapes=[pltpu.VMEM((G, N, inner // G), F32)],
        compiler_params=_cparams("parallel", "arbitrary"),
        name="ssd_scan",
    )(xc, dt, dt_bias, a_log, d_skip)


OUT_ROW_BLOCKS = 2


def _ssd_out_kernel(y_ref, z_ref, ng_ref, w_ref, h_ref, g_ref, o_ref):
    tm = y_ref.shape[0]
    rb = tm // OUT_ROW_BLOCKS
    for r0 in range(0, tm, rb):
        rs = slice(r0, r0 + rb)
        z = z_ref[rs, :].astype(F32)
        yg = y_ref[rs, :].astype(F32) * (z * jax.nn.sigmoid(z))
        m = _dot(_rms(yg, ng_ref[...]).astype(BF16), w_ref[...])
        o_ref[rs, :] = h_ref[rs, :] + _rms(m, g_ref[...])


def _ssd_out(y, z, norm_g, w_out, h, g):
    t, d = h.shape
    inner = y.shape[1]
    tm = min(TOKEN_TILE, t)
    row = lambda n: pl.BlockSpec((tm, n), lambda i: (i, 0))
    return pl.pallas_call(
        _ssd_out_kernel,
        grid=(t // tm,),
        in_specs=[row(inner), row(inner), _const_spec((1, inner)), _const_spec(w_out.shape), row(d),
                  _const_spec((1, d))],
        out_specs=row(d),
        out_shape=jax.ShapeDtypeStruct((t, d), F32),
        compiler_params=_cparams("parallel"),
        name="ssd_out",
    )(y, z, norm_g, w_out, h, g)


def _pad_lanes(v):
    return jnp.zeros((1, LANES), F32).at[0, :v.shape[0]].set(v.astype(F32))


def _prep_ssd(w_in, conv_w, conv_b, dt_bias, a_log, d_skip, norm_g, w_out):
    d, n_in = w_in.shape
    n_heads = dt_bias.shape[0]
    inner = n_heads * SSD_HEADDIM
    cc = conv_w.shape[1]
    w = jnp.zeros((d, inner + cc + LANES), BF16).at[:, :n_in].set(w_in.astype(BF16))
    return (w, conv_w.astype(F32), conv_b.astype(F32)[None], _pad_lanes(dt_bias), _pad_lanes(a_log),
            jnp.repeat(d_skip.astype(F32), SSD_HEADDIM)[None], norm_g.astype(F32)[None], w_out.astype(BF16))


def _ssd_layer(h, g_pre, g_post, w, conv_w, conv_b, dt_bias, a_log, d_skip, norm_g, w_out):
    b, s, d = h.shape
    inner = norm_g.shape[-1]
    cc = conv_w.shape[1]
    z, xc, dt = _ssd_proj(h, g_pre, w, conv_w, conv_b, inner, cc)
    y = _ssd_scan(xc, dt, dt_bias, a_log, d_skip, inner)
    return _ssd_out(y.reshape(b * s, inner), z.reshape(b * s, inner), norm_g, w_out, h.reshape(b * s, d),
                    g_post).reshape(b, s, d)


SB_HEADS = 8
NSA_HEADS = 8
NSA_GROUPS = 2
NSA_REP = NSA_HEADS // NSA_GROUPS
CMP_LEN = 32
CMP_STRIDE = 16
SEL_BLOCK = 64
SEL_TOPN = 16
WINDOW = 512
FORCE_BONUS = 1e4
NSA_QB = 128
SB_W = SB_HEADS * HEAD_DIM
NSA_QW = NSA_HEADS * HEAD_DIM
NSA_KVW = NSA_GROUPS * HEAD_DIM
ATT_SCALE = HEAD_DIM ** -0.5 * LOG2E

_C_QSB, _C_KSB, _C_VSB = 0, SB_W, 2 * SB_W
_C_QN = 3 * SB_W
_C_QNP = _C_QN + NSA_QW
_C_K3 = _C_QNP + NSA_QW
_C_K3P = _C_K3 + 3 * NSA_KVW
_C_V3 = _C_K3P + 3 * NSA_KVW
_C_GATE = _C_V3 + 3 * NSA_KVW
_C_END = _C_GATE + LANES


def _hyb_proj_kernel(h_ref, g_ref, pos_ref, inv_ref, sgn_ref, w_ref,
                     qsb_ref, ksb_ref, vsb_ref, qn_ref, kcmp_ref, vcmp_ref,
                     ksel_ref, vsel_ref, kwin_ref, vwin_ref, gate_ref):
    xn = _rms(h_ref[...], g_ref[...]).astype(BF16)
    tm = xn.shape[0]
    hd = HEAD_DIM

    def proj(lo, n):
        return _dot(xn, w_ref[:, lo:lo + n])

    for ref, lo, scale in ((qsb_ref, _C_QSB, ATT_SCALE), (ksb_ref, _C_KSB, 1.0), (vsb_ref, _C_VSB, 1.0)):
        p = proj(lo, SB_W) * scale
        for a in range(SB_HEADS):
            ref[a] = p[:, a * hd:(a + 1) * hd].astype(BF16)

    ang = pos_ref[...] * inv_ref[...]
    cos = jnp.cos(ang)
    sin = jnp.sin(ang) * sgn_ref[...]

    def rot(lo, lop, n):
        reps = n // LANES
        return (proj(lo, n) * jnp.concatenate([cos] * reps, axis=1)
                + proj(lop, n) * jnp.concatenate([sin] * reps, axis=1))

    qn = rot(_C_QN, _C_QNP, NSA_QW) * ATT_SCALE
    lane = lax.broadcasted_iota(jnp.int32, (tm, LANES), 1)
    low = lane < hd
    for a in range(NSA_HEADS):
        grp = a // NSA_REP
        piece = qn[:, (a // 2) * LANES:(a // 2 + 1) * LANES]
        if (a % 2) != grp:
            piece = pltpu.roll(piece, hd, 1)
        keep = low if grp == 0 else jnp.logical_not(low)
        qn_ref[a] = jnp.where(keep, piece, 0.0).astype(BF16)

    k3 = rot(_C_K3, _C_K3P, 3 * NSA_KVW)
    v3 = proj(_C_V3, 3 * NSA_KVW)
    for grp in range(NSA_GROUPS):
        kcmp_ref[grp] = k3[:, grp * hd:(grp + 1) * hd].astype(BF16)
        vcmp_ref[grp] = v3[:, grp * hd:(grp + 1) * hd].astype(BF16)
    ksel_ref[...] = k3[:, NSA_KVW:2 * NSA_KVW].astype(BF16)
    vsel_ref[...] = v3[:, NSA_KVW:2 * NSA_KVW].astype(BF16)
    kwin_ref[...] = k3[:, 2 * NSA_KVW:].astype(BF16)
    vwin_ref[...] = v3[:, 2 * NSA_KVW:].astype(BF16)
    gate_ref[...] = jax.nn.sigmoid(proj(_C_GATE, LANES))


def _hyb_proj(h, g, posf, inv, sgn, w):
    b, s, d = h.shape
    tm = min(TOKEN_TILE, s)
    heads = lambda n, w_: pl.BlockSpec((None, n, tm, w_), lambda i, j: (i, 0, j, 0))
    rows = lambda w_: pl.BlockSpec((None, tm, w_), lambda i, j: (i, j, 0))
    hshape = lambda n, w_: jax.ShapeDtypeStruct((b, n, s, w_), BF16)
    rshape = lambda w_, dt: jax.ShapeDtypeStruct((b, s, w_), dt)
    return pl.pallas_call(
        _hyb_proj_kernel,
        grid=(b, s // tm),
        in_specs=[rows(d), _const_spec((1, d)), rows(1), _const_spec((1, LANES)), _const_spec((1, LANES)),
                  _const_spec(w.shape)],
        out_specs=[heads(SB_HEADS, HEAD_DIM)] * 3 + [heads(NSA_HEADS, LANES)]
        + [heads(NSA_GROUPS, HEAD_DIM)] * 2 + [rows(LANES)] * 5,
        out_shape=[hshape(SB_HEADS, HEAD_DIM)] * 3 + [hshape(NSA_HEADS, LANES)]
        + [hshape(NSA_GROUPS, HEAD_DIM)] * 2 + [rshape(LANES, BF16)] * 4 + [rshape(LANES, F32)],
        compiler_params=_cparams("parallel", "parallel"),
        name="hyb_proj",
    )(h, g, posf, inv, sgn, w)


SB_TILE = 256
SB_SUB = 128
SB_DEAD_LOG2 = 160.0


def _sb_blocks(ops, later2, mask):
    ts = [_dot_nt(q, k) for q, k, _, _ in ops]
    sps, hls = [], []
    for t in ts:
        neg_abs = lax.bitcast_convert_type(lax.bitcast_convert_type(t, jnp.uint32) | jnp.uint32(0x80000000),
                                           F32)
        sp = jnp.maximum(t, 0.0) + jnp.log2(1.0 + jnp.exp2(neg_abs))
        if mask is not None:
            sp = jnp.where(mask, sp, 0.0)
        hi = sp.astype(BF16)
        lo = (sp - hi.astype(F32)).astype(BF16)
        sps.append(sp)
        hls.append(jnp.concatenate([hi, lo], axis=1))
    css = [_dot(hl, later2) for hl in hls]
    ws = []
    for t, sp, cs, (_, _, _, carry) in zip(ts, sps, css, ops):
        w = jnp.exp2(t - sp - cs - carry)
        if mask is not None:
            w = jnp.where(mask, w, 0.0)
        ws.append(w.astype(BF16))
    return [(_dot(w, v), carry + cs[:, 0:1] + sp[:, 0:1])
            for w, sp, cs, (_, _, v, carry) in zip(ws, sps, css, ops)]


def _sb_kernel(q_ref, kd_ref, vd_ref, k_hbm, v_hbm, o_ref, acc_ref, carry_ref, kbuf, vbuf, sem):
    bi = pl.program_id(0)
    i = pl.program_id(1)
    n_heads, tq, hd = q_ref.shape
    sub = kbuf.shape[2]
    n_diag = tq // sub
    n_prev = i * n_diag

    def copies(n, slot):
        start = pl.multiple_of((n_prev - 1 - n) * sub, sub)
        return (pltpu.make_async_copy(k_hbm.at[bi, :, pl.ds(start, sub), :], kbuf.at[slot], sem.at[0, slot]),
                pltpu.make_async_copy(v_hbm.at[bi, :, pl.ds(start, sub), :], vbuf.at[slot], sem.at[1, slot]))

    def start_fetch(n, slot):
        for c in copies(n, slot):
            c.start()

    def wait_fetch(n, slot):
        for c in copies(n, slot):
            c.wait()

    @pl.when(n_prev > 0)
    def _():
        start_fetch(0, 0)

    kr = lax.broadcasted_iota(jnp.int32, (2 * sub, sub), 0) % sub
    kc = lax.broadcasted_iota(jnp.int32, (2 * sub, sub), 1)
    later2 = jnp.where(kr > kc, 1.0, 0.0).astype(BF16)

    acc_ref[...] = jnp.zeros(acc_ref.shape, F32)
    carry_ref[...] = jnp.zeros(carry_ref.shape, F32)

    for sb in reversed(range(n_diag)):
        r0 = sb * sub
        rows = tq - r0
        mask = (lax.broadcasted_iota(jnp.int32, (rows, sub), 1)
                < lax.broadcasted_iota(jnp.int32, (rows, sub), 0))
        ops = [(q_ref[a, r0:, :], kd_ref[a, r0:r0 + sub, :], vd_ref[a, r0:r0 + sub, :], carry_ref[a, r0:, :])
               for a in range(n_heads)]
        for a, (pv, carry) in enumerate(_sb_blocks(ops, later2, mask)):
            acc_ref[a, r0:, :] += pv
            carry_ref[a, r0:, :] = carry

    def live():
        return jnp.min(carry_ref[...]) < SB_DEAD_LOG2

    def body(state):
        n, _ = state
        slot = n % 2
        wait_fetch(n, slot)

        @pl.when(n + 1 < n_prev)
        def _():
            start_fetch(n + 1, 1 - slot)

        ops = [(q_ref[a], kbuf[slot, a], vbuf[slot, a], carry_ref[a]) for a in range(n_heads)]
        for a, (pv, carry) in enumerate(_sb_blocks(ops, later2, None)):
            acc_ref[a] += pv
            carry_ref[a] = carry
        return n + 1, live()

    n_end, _ = lax.while_loop(lambda st: (st[0] < n_prev) & st[1], body, (jnp.int32(0), live()))

    @pl.when(n_end < n_prev)
    def _():
        wait_fetch(n_end, n_end % 2)

    for a in range(n_heads):
        o_ref[:, a * hd:(a + 1) * hd] = acc_ref[a].astype(o_ref.dtype)


def _sb_attn(q, k, v):
    b, n_heads, s, hd = q.shape
    t = min(SB_TILE, s)
    sub = min(SB_SUB, t)
    tile = pl.BlockSpec((None, n_heads, t, hd), lambda bi, i: (bi, 0, i, 0))
    hbm = pl.BlockSpec(memory_space=pl.ANY)
    return pl.pallas_call(
        _sb_kernel,
        grid=(b, s // t),
        in_specs=[tile, tile, tile, hbm, hbm],
        out_specs=pl.BlockSpec((None, t, n_heads * hd), lambda bi, i: (bi, i, 0)),
        out_shape=jax.ShapeDtypeStruct((b, s, n_heads * hd), BF16),
        scratch_shapes=[pltpu.VMEM((n_heads, t, hd), F32), pltpu.VMEM((n_heads, t, 1), F32),
                        pltpu.VMEM((2, n_heads, sub, hd), BF16), pltpu.VMEM((2, n_heads, sub, hd), BF16),
                        pltpu.SemaphoreType.DMA((2, 2))],
        compiler_params=_cparams("parallel", "arbitrary"),
        name="sb_attn",
    )(q, k, v, k, v)


def _compress_kernel(t_ref, wab_ref, pe_ref, w1_ref, w2_ref, o_ref):
    n_grp, nch, _ = t_ref.shape
    hid = w2_ref.shape[0]
    pe8 = jnp.broadcast_to(pe_ref[...], (8, pe_ref.shape[1])).astype(BF16)
    bias = _dot(pe8, w1_ref[...])[0:1, :]
    for grp in range(n_grp):
        ab = _dot(t_ref[grp], wab_ref[...])
        nxt = pltpu.roll(ab[:, hid:], nch - 1, 0)
        hcur = ab[:, :hid] + nxt + bias
        act = (hcur * jax.nn.sigmoid(hcur)).astype(BF16)
        o_ref[:, grp * HEAD_DIM:(grp + 1) * HEAD_DIM] = _dot(act, w2_ref[...]).astype(o_ref.dtype)


def _compress(t, wab, pe, w1, w2):
    b, n_grp, nch, kw = t.shape
    return pl.pallas_call(
        _compress_kernel,
        grid=(b,),
        in_specs=[pl.BlockSpec((None, n_grp, nch, kw), lambda i: (i, 0, 0, 0)), _const_spec(wab.shape),
                  _const_spec(pe.shape), _const_spec(w1.shape), _const_spec(w2.shape)],
        out_specs=pl.BlockSpec((None, nch, n_grp * HEAD_DIM), lambda i: (i, 0, 0)),
        out_shape=jax.ShapeDtypeStruct((b, nch, n_grp * HEAD_DIM), BF16),
        compiler_params=_cparams("parallel"),
        name="nsa_compress",
    )(t, wab, pe, w1, w2)


def _masked_softmax_rows(s, mask):
    sm = jnp.where(mask, s, NEG_BIG)
    m = jnp.max(sm, axis=-1, keepdims=True)
    e = jnp.where(mask, jnp.exp2(sm - m), 0.0)
    den = jnp.sum(e, axis=-1, keepdims=True)
    return e / jnp.maximum(den, 1e-30)


def _softmax_num(s, bias, groups):
    rows, n = s.shape
    sm = (s.reshape(groups, rows // groups, n) + bias[None]).reshape(rows, n)
    e = jnp.exp2(sm - jnp.max(sm, axis=-1, keepdims=True))
    return e.astype(BF16), jnp.sum(e, axis=-1, keepdims=True)


RANK_UNROLL = 8


def _select_kernel(qf_ref, kc_ref, o_ref):
    n_heads, nsub, _ = qf_ref.shape
    nch = kc_ref.shape[0]
    q = qf_ref[...].reshape(n_heads * nsub, LANES)
    s = _dot_nt(q, kc_ref[...])
    sub_id = lax.broadcasted_iota(jnp.int32, (n_heads * nsub, nch), 0) % nsub
    c_id = lax.broadcasted_iota(jnp.int32, (n_heads * nsub, nch), 1)
    p = _masked_softmax_rows(s, c_id * CMP_STRIDE + (CMP_LEN - 1) <= sub_id * SEL_BLOCK)
    rows = NSA_GROUPS * nsub
    psum = jnp.concatenate(
        [sum(p[(grp * NSA_REP + r) * nsub:(grp * NSA_REP + r + 1) * nsub] for r in range(NSA_REP))
         for grp in range(NSA_GROUPS)], axis=0)
    oc = lax.broadcasted_iota(jnp.int32, (nch, LANES), 0) * CMP_STRIDE
    oj = lax.broadcasted_iota(jnp.int32, (nch, LANES), 1) * SEL_BLOCK
    overlap = jnp.where((oc < oj + SEL_BLOCK) & (oc + (CMP_LEN - 1) >= oj), 1.0, 0.0).astype(BF16)
    p1, p2, p3 = _split3(psum)
    imp = _dot(p1, overlap) + _dot(p2, overlap) + _dot(p3, overlap)
    jl = lax.broadcasted_iota(jnp.int32, (rows, LANES), 1)
    cur = lax.broadcasted_iota(jnp.int32, (rows, LANES), 0) % nsub
    forced = (jl == 0) | (jl == cur) | (jl == cur - 1)
    score = jnp.where(jl <= cur, imp + jnp.where(forced, FORCE_BONUS, 0.0), -1.0)

    def body(d, rank):
        other = pltpu.roll(score, d, 1)
        ahead = (other > score) | ((other == score) & (jl >= d))
        return rank + jnp.where(ahead, 1.0, 0.0)

    rank = lax.fori_loop(1, LANES, body, jnp.zeros((rows, LANES), F32), unroll=RANK_UNROLL)
    jf = jl.astype(F32)
    out = jnp.zeros((rows, LANES), F32)
    for r in range(SEL_TOPN):
        col = jnp.sum(jnp.where(rank == float(r), jf, 0.0), axis=-1, keepdims=True)
        out = jnp.where(jl == r, col, out)
    o_ref[...] = out.astype(jnp.int32)


def _nsa_select(qf, kc):
    b, n_heads, nsub, _ = qf.shape
    nch = kc.shape[1]
    assert SEL_TOPN <= nsub <= LANES
    return pl.pallas_call(
        _select_kernel,
        grid=(b,),
        in_specs=[pl.BlockSpec((None, n_heads, nsub, LANES), lambda i: (i, 0, 0, 0)),
                  pl.BlockSpec((None, nch, LANES), lambda i: (i, 0, 0))],
        out_specs=pl.BlockSpec((None, NSA_GROUPS * nsub, LANES), lambda i: (i, 0, 0)),
        out_shape=jax.ShapeDtypeStruct((b, NSA_GROUPS * nsub, LANES), jnp.int32),
        compiler_params=_cparams("parallel"),
        name="nsa_select",
    )(qf, kc)


def _nsa_kernel(sel_ref, qn_ref, kc_ref, vc_ref, ksel_ref, vsel_ref, kwin_ref, vwin_ref, gate_ref,
                o_ref, kg_ref, vg_ref):
    bi = pl.program_id(0)
    i = pl.program_id(1)
    n_heads, qb, _ = qn_ref.shape
    s_len = ksel_ref.shape[0]
    nsub_total = s_len // SEL_BLOCK
    nch = kc_ref.shape[0]
    hd = HEAD_DIM
    q = qn_ref[...].reshape(n_heads * qb, LANES)
    qpos = i * qb + lax.broadcasted_iota(jnp.int32, (qb, 1), 0)

    def bias_of(mask):
        return jnp.where(mask, 0.0, NEG_BIG)

    nsel = SEL_TOPN * SEL_BLOCK
    lane_blk = lax.broadcasted_iota(jnp.int32, (1, nsel), 1) // SEL_BLOCK
    lane_off = lax.broadcasted_iota(jnp.int32, (1, nsel), 1) % SEL_BLOCK
    n_sub = qb // SEL_BLOCK
    sel_jobs = []
    for grp in range(NSA_GROUPS):
        for n in range(n_sub):
            slot = grp * n_sub + n
            base = ((bi * NSA_GROUPS + grp) * nsub_total + i * n_sub + n) * SEL_TOPN
            tok = lane_off
            for t in range(SEL_TOPN):
                idx = sel_ref[base + t]
                off = pl.multiple_of(idx * SEL_BLOCK, SEL_BLOCK)
                kg_ref[slot, t * SEL_BLOCK:(t + 1) * SEL_BLOCK, :] = ksel_ref[pl.ds(off, SEL_BLOCK), :]
                vg_ref[slot, t * SEL_BLOCK:(t + 1) * SEL_BLOCK, :] = vsel_ref[pl.ds(off, SEL_BLOCK), :]
                tok = tok + jnp.where(lane_blk == t, idx * SEL_BLOCK, 0)
            qs = qn_ref[grp * NSA_REP:(grp + 1) * NSA_REP, n * SEL_BLOCK:(n + 1) * SEL_BLOCK, :]
            qp = i * qb + n * SEL_BLOCK + lax.broadcasted_iota(jnp.int32, (SEL_BLOCK, 1), 0)
            sel_jobs.append((slot, qs.reshape(NSA_REP * SEL_BLOCK, LANES), bias_of(tok <= qp)))

    c_end = lax.broadcasted_iota(jnp.int32, (1, nch), 1) * CMP_STRIDE + (CMP_LEN - 1)
    wlen = WINDOW + qb
    start = pl.multiple_of(jnp.maximum(i * qb - WINDOW, 0), qb)
    delta = qpos - (start + lax.broadcasted_iota(jnp.int32, (1, wlen), 1))

    s_c = _dot_nt(q, kc_ref[...])
    s_w = _dot_nt(q, kwin_ref[pl.ds(start, wlen), :])
    s_s = [_dot_nt(qs, kg_ref[slot]) for slot, qs, _ in sel_jobs]

    e_c, den_c = _softmax_num(s_c, bias_of(c_end <= qpos), n_heads)
    e_w, den_w = _softmax_num(s_w, bias_of((delta >= 0) & (delta < WINDOW)), n_heads)
    num_s = [_softmax_num(s, bias, NSA_REP) for s, (_, _, bias) in zip(s_s, sel_jobs)]

    any_c = jnp.where(qpos >= CMP_LEN - 1, 1.0, 0.0)
    o_c = _dot(e_c, vc_ref[...]) / den_c
    o_c = (o_c.reshape(n_heads, qb, LANES) * any_c[None]).reshape(n_heads * qb, LANES)
    o_w = _dot(e_w, vwin_ref[pl.ds(start, wlen), :]) / den_w
    o_sel = [_dot(e, vg_ref[slot]) / den for (e, den), (slot, _, _) in zip(num_s, sel_jobs)]
    o_s = [[o_sel[grp * n_sub + n] for n in range(n_sub)] for grp in range(NSA_GROUPS)]


    lane = lax.broadcasted_iota(jnp.int32, (qb, LANES), 1)
    low = lane < hd
    gates = gate_ref[...]
    mixed = []
    for a in range(n_heads):
        grp, r = a // NSA_REP, a % NSA_REP
        rs = slice(a * qb, (a + 1) * qb)
        sel_rows = jnp.concatenate([o_s[grp][n][r * SEL_BLOCK:(r + 1) * SEL_BLOCK] for n in range(n_sub)],
                                   axis=0)
        mixed.append(gates[:, 3 * a:3 * a + 1] * o_c[rs] + gates[:, 3 * a + 1:3 * a + 2] * sel_rows
                     + gates[:, 3 * a + 2:3 * a + 3] * o_w[rs])
    for pr in range(n_heads // 2):
        grp = (2 * pr) // NSA_REP
        left, right = mixed[2 * pr], mixed[2 * pr + 1]
        if grp == 0:
            right = pltpu.roll(right, hd, 1)
        else:
            left = pltpu.roll(left, hd, 1)
        o_ref[:, pr * LANES:(pr + 1) * LANES] = jnp.where(low, left, right).astype(o_ref.dtype)


def _nsa_attn(sel, qn, kc, vc, ksel, vsel, kwin, vwin, gates):
    b, n_heads, s, _ = qn.shape
    nch = kc.shape[1]
    qb = min(NSA_QB, s)
    assert s >= WINDOW + qb
    full = lambda n: pl.BlockSpec((None, n, LANES), lambda bi, i, sel: (bi, 0, 0))
    grid_spec = pltpu.PrefetchScalarGridSpec(
        num_scalar_prefetch=1,
        grid=(b, s // qb),
        in_specs=[pl.BlockSpec((None, n_heads, qb, LANES), lambda bi, i, sel: (bi, 0, i, 0)),
                  full(nch), full(nch), full(s), full(s), full(s), full(s),
                  pl.BlockSpec((None, qb, LANES), lambda bi, i, sel: (bi, i, 0))],
        out_specs=pl.BlockSpec((None, qb, n_heads * HEAD_DIM), lambda bi, i, sel: (bi, i, 0)),
        scratch_shapes=[pltpu.VMEM((NSA_GROUPS * (qb // SEL_BLOCK), SEL_TOPN * SEL_BLOCK, LANES), BF16)] * 2,
    )
    return pl.pallas_call(
        _nsa_kernel,
        grid_spec=grid_spec,
        out_shape=jax.ShapeDtypeStruct((b, s, n_heads * HEAD_DIM), BF16),
        compiler_params=_cparams("parallel", "arbitrary"),
        name="nsa_attn",
    )(sel, qn, kc, vc, ksel, vsel, kwin, vwin, gates)


def _prep_hybrid(w_in, w_out, pe_k, w1_k, w2_k, pe_v, w1_v, w2_v):
    d = w_in.shape[0]
    hd = HEAD_DIM
    col = np.arange(w_in.shape[1])
    o_qn = 3 * SB_W
    o_kv = o_qn + NSA_QW
    kv = lambda n: col[o_kv + n * NSA_KVW:o_kv + (n + 1) * NSA_KVW]
    partner = lambda c: (c // hd) * hd + (c % hd + hd // 2) % hd
    qn_c = col[o_qn:o_qn + NSA_QW]
    k3_c = np.concatenate([kv(0), kv(2), kv(4)])
    v3_c = np.concatenate([kv(1), kv(3), kv(5)])
    gate_c = col[o_kv + 6 * NSA_KVW:]
    order = np.concatenate([col[:o_qn], qn_c, partner(qn_c), k3_c, partner(k3_c), v3_c, gate_c])
    w = jnp.zeros((d, _C_END), BF16).at[:, :order.shape[0]].set(w_in[:, order].astype(BF16))
    lanes = np.arange(LANES)
    inv = (ROPE_THETA ** (-(lanes % ROPE_HALF).astype(np.float32) / ROPE_HALF)).astype(np.float32)[None]
    sgn = np.where(lanes % hd < ROPE_HALF, -1.0, 1.0).astype(np.float32)[None]

    def cmp_w(pe, w1, w2):
        half = w1.shape[0] // 2
        wab = jnp.concatenate([w1[:half], w1[half:]], axis=1).astype(BF16)
        return wab, pe.reshape(1, -1).astype(F32), w1.astype(BF16), w2.astype(BF16)

    return (w, jnp.asarray(inv), jnp.asarray(sgn), w_out[:SB_W].astype(BF16), w_out[SB_W:].astype(BF16),
            cmp_w(pe_k, w1_k, w2_k), cmp_w(pe_v, w1_v, w2_v))


def _hybrid_layer(h, positions, g_pre, g_post, w, inv, sgn, wo_sb, wo_nsa, cmp_k, cmp_v):
    b, s, d = h.shape
    posf = positions.astype(F32)[..., None]
    (qsb, ksb, vsb, qn, kcmp, vcmp, ksel, vsel, kwin, vwin, gates) = _hyb_proj(h, g_pre, posf, inv, sgn, w)
    o_sb = _sb_attn(qsb, ksb, vsb)
    half = CMP_LEN // 2
    chunks = lambda t: t.reshape(b, NSA_GROUPS, s // half, half * HEAD_DIM)
    kc = _compress(chunks(kcmp), *cmp_k)
    vc = _compress(chunks(vcmp), *cmp_v)
    sel = _nsa_select(qn[:, :, ::SEL_BLOCK, :], kc)
    sel = sel[:, :, :SEL_TOPN].reshape(-1)
    o_nsa = _nsa_attn(sel, qn, kc, vc, ksel, vsel, kwin, vwin, gates)
    hf = h.reshape(b * s, d)
    return _outproj([o_sb.reshape(b * s, SB_W), o_nsa.reshape(b * s, NSA_QW)], [wo_sb, wo_nsa], hf,
                    g_post).reshape(b, s, d)


CROSS_HEADS = 4


def kernel(x, mem, positions, norm_g, ffn1_w_gu, ffn1_w_down, ffn2_w_gu, ffn2_w_down, cross_wq, cross_wkv,
           cross_wo, hyb_w_in, hyb_w_out, cmp_pe_k, cmp_w1_k, cmp_w2_k, cmp_pe_v, cmp_w1_v, cmp_w2_v,
           ssd_w_in, ssd_conv_w, ssd_conv_b, ssd_dt_bias, ssd_A_log, ssd_D, ssd_norm_g, ssd_w_out):
    b, s, d = x.shape
    depth = norm_g.shape[0]
    h = x
    for i in range(depth):
        g = norm_g[i].astype(F32)[:, None, :]
        j = i // 2
        h = _ffn(h.reshape(b * s, d), g[0], g[1], ffn1_w_gu[i].astype(BF16),
                 ffn1_w_down[i].astype(BF16)).reshape(b, s, d)
        if i % 2 == 0:
            h = _hybrid_layer(h, positions, g[2], g[3],
                              *_prep_hybrid(hyb_w_in[j], hyb_w_out[j], cmp_pe_k[j], cmp_w1_k[j], cmp_w2_k[j],
                                            cmp_pe_v[j], cmp_w1_v[j], cmp_w2_v[j]))
        else:
            h = _ssd_layer(h, g[2], g[3], *_prep_ssd(ssd_w_in[j], ssd_conv_w[j], ssd_conv_b[j], ssd_dt_bias[j],
                                                     ssd_A_log[j], ssd_D[j], ssd_norm_g[j], ssd_w_out[j]))
        k, v = _memkv(mem, g[6], cross_wkv[i].astype(BF16))
        h = _cross(h, g[4], g[5], cross_wq[i].astype(BF16), k, v, cross_wo[i].astype(BF16), CROSS_HEADS)
        h = _ffn(h.reshape(b * s, d), g[7], g[8], ffn2_w_gu[i].astype(BF16),
                 ffn2_w_down[i].astype(BF16)).reshape(b, s, d)
    return h
```

```python
import functools
import math

import jax
import jax.numpy as jnp
import numpy as np
from jax import lax
from jax.experimental import pallas as pl
from jax.experimental.pallas import tpu as pltpu

F32 = jnp.float32
BF16 = jnp.bfloat16

RMS_EPS = 1e-6
LOG2E = 1.4426950408889634
ROPE_THETA = 10000.0
HEAD_DIM = 64
ROPE_HALF = HEAD_DIM // 2

V7X_VMEM_BYTES = 64 * 1024 * 1024
VMEM_LIMIT = (V7X_VMEM_BYTES * 3) // 4
LANES = 128

TOKEN_TILE = 512
FF_CHUNK = 512


def _cparams(*sem):
    return pltpu.CompilerParams(dimension_semantics=sem, vmem_limit_bytes=VMEM_LIMIT)


def _rms(x, g):
    ms = jnp.mean(x * x, axis=-1, keepdims=True)
    return x * lax.rsqrt(ms + RMS_EPS) * g


def _dot(a, b):
    return jnp.dot(a, b, preferred_element_type=F32)


def _dot_nt(a, b):
    return lax.dot_general(a, b, (((1,), (1,)), ((), ())), preferred_element_type=F32)


def _const_spec(shape):
    nd = len(shape)
    return pl.BlockSpec(shape, lambda *_: (0,) * nd)


def _ffn_kernel(h_ref, gpre_ref, gpost_ref, wgu_ref, wd_ref, o_ref):
    d_ff = wd_ref.shape[0]
    h = h_ref[...]
    xn = _rms(h, gpre_ref[...]).astype(BF16)
    acc = jnp.zeros(h.shape, F32)
    for c in range(d_ff // FF_CHUNK):
        lo = c * FF_CHUNK
        gate = _dot(xn, wgu_ref[:, lo:lo + FF_CHUNK])
        up = _dot(xn, wgu_ref[:, d_ff + lo:d_ff + lo + FF_CHUNK])
        act = (gate * jax.nn.sigmoid(gate) * up).astype(BF16)
        acc = acc + _dot(act, wd_ref[lo:lo + FF_CHUNK, :])
    o_ref[...] = h + 0.5 * _rms(acc, gpost_ref[...])


def _ffn(h, g_pre, g_post, w_gu, w_down):
    t, d = h.shape
    d_ff = w_down.shape[0]
    tm = min(TOKEN_TILE, t)
    return pl.pallas_call(
        _ffn_kernel,
        grid=(t // tm,),
        in_specs=[
            pl.BlockSpec((tm, d), lambda i: (i, 0)),
            _const_spec((1, d)),
            _const_spec((1, d)),
            _const_spec((d, 2 * d_ff)),
            _const_spec((d_ff, d)),
        ],
        out_specs=pl.BlockSpec((tm, d), lambda i: (i, 0)),
        out_shape=jax.ShapeDtypeStruct((t, d), F32),
        compiler_params=_cparams("parallel"),
        name="ffn",
    )(h, g_pre, g_post, w_gu, w_down)


def _memkv_kernel(mem_ref, g_ref, wkv_ref, k_ref, v_ref):
    d = mem_ref.shape[-1]
    mn = _rms(mem_ref[...], g_ref[...]).astype(BF16)
    kv = _dot(mn, wkv_ref[...])
    k_ref[...] = kv[:, :d].astype(BF16)
    v_ref[...] = kv[:, d:].astype(BF16)


def _memkv(mem, g_mem, wkv):
    b, m, d = mem.shape
    return pl.pallas_call(
        _memkv_kernel,
        grid=(b,),
        in_specs=[
            pl.BlockSpec((None, m, d), lambda i: (i, 0, 0)),
            _const_spec((1, d)),
            _const_spec((d, 2 * d)),
        ],
        out_specs=[pl.BlockSpec((None, m, d), lambda i: (i, 0, 0))] * 2,
        out_shape=[jax.ShapeDtypeStruct((b, m, d), BF16)] * 2,
        compiler_params=_cparams("parallel"),
        name="memkv",
    )(mem, g_mem, wkv)


def _cross_kernel(n_heads, h_ref, gpre_ref, gpost_ref, wq_ref, k_ref, v_ref, wo_ref, o_ref):
    h = h_ref[...]
    d = h.shape[-1]
    hd = d // n_heads
    hn = _rms(h, gpre_ref[...]).astype(BF16)
    q = (_dot(hn, wq_ref[...]) * (hd ** -0.5 * LOG2E)).astype(BF16)
    sls = [slice(a * hd, (a + 1) * hd) for a in range(n_heads)]
    ss = [_dot_nt(q[:, sl], k_ref[:, sl]) for sl in sls]
    es = [jnp.exp2(s - jnp.max(s, axis=-1, keepdims=True)) for s in ss]
    os_ = [(_dot(e.astype(BF16), v_ref[:, sl]) / jnp.sum(e, axis=-1, keepdims=True)).astype(BF16)
           for e, sl in zip(es, sls)]
    c = _dot(jnp.concatenate(os_, axis=1), wo_ref[...])
    o_ref[...] = h + _rms(c, gpost_ref[...])


def _cross(h, g_pre, g_post, wq, k, v, wo, n_heads):
    b, s, d = h.shape
    m = k.shape[1]
    tm = min(TOKEN_TILE, s)
    return pl.pallas_call(
        functools.partial(_cross_kernel, n_heads),
        grid=(b, s // tm),
        in_specs=[
            pl.BlockSpec((None, tm, d), lambda i, j: (i, j, 0)),
            _const_spec((1, d)),
            _const_spec((1, d)),
            _const_spec((d, d)),
            pl.BlockSpec((None, m, d), lambda i, j: (i, 0, 0)),
            pl.BlockSpec((None, m, d), lambda i, j: (i, 0, 0)),
            _const_spec((d, d)),
        ],
        out_specs=pl.BlockSpec((None, tm, d), lambda i, j: (i, j, 0)),
        out_shape=jax.ShapeDtypeStruct((b, s, d), F32),
        compiler_params=_cparams("parallel", "parallel"),
        name="cross",
    )(h, g_pre, g_post, wq, k, v, wo)


def _outproj_kernel(n_in, *refs):
    a_refs = refs[:n_in]
    w_refs = refs[n_in:2 * n_in]
    h_ref, g_ref, o_ref = refs[2 * n_in:]
    m = _dot(a_refs[0][...], w_refs[0][...])
    for a_ref, w_ref in zip(a_refs[1:], w_refs[1:]):
        m = m + _dot(a_ref[...], w_ref[...])
    o_ref[...] = h_ref[...] + _rms(m, g_ref[...])


def _outproj(acts, weights, h, g):
    t, d = h.shape
    tm = min(TOKEN_TILE, t)
    n_in = len(acts)
    return pl.pallas_call(
        functools.partial(_outproj_kernel, n_in),
        grid=(t // tm,),
        in_specs=(
            [pl.BlockSpec((tm, a.shape[1]), lambda i: (i, 0)) for a in acts]
            + [_const_spec(w.shape) for w in weights]
            + [pl.BlockSpec((tm, d), lambda i: (i, 0)), _const_spec((1, d))]
        ),
        out_specs=pl.BlockSpec((tm, d), lambda i: (i, 0)),
        out_shape=jax.ShapeDtypeStruct((t, d), F32),
        compiler_params=_cparams("parallel"),
        name="outproj",
    )(*acts, *weights, h, g)


SSD_HEADDIM = 64
SSD_GROUPS = 4
SSD_STATE = 128
SSD_CONV = 4
SSD_L = 128
CONV_HALO = 16
NEG_BIG = -1e30


def _split3(x):
    a = x.astype(BF16)
    r = x - a.astype(F32)
    b = r.astype(BF16)
    c = (r - b.astype(F32)).astype(BF16)
    return a, b, c


def _softplus(x):
    return jnp.maximum(x, 0.0) + jnp.log(1.0 + jnp.exp(-jnp.abs(x)))


CONV_CHUNK = 256


def _ssd_proj_kernel(h_ref, halo_ref, g_ref, w_ref, cw_ref, cb_ref, z_ref, xc_ref, dt_ref, xe_ref):
    tm = h_ref.shape[0]
    inner = z_ref.shape[-1]
    cc = xc_ref.shape[-1]
    H0 = CONV_HALO
    n_sh = SSD_CONV - 1
    xe_ref[0:H0, :] = jnp.where(pl.program_id(1) > 0, _rms(halo_ref[...], g_ref[...]), 0.0).astype(BF16)
    xe_ref[H0:, :] = _rms(h_ref[...], g_ref[...]).astype(BF16)
    z_chunks = list(range(0, inner, FF_CHUNK))
    conv_chunks = list(range(0, cc, CONV_CHUNK))
    per_z = -(-len(conv_chunks) // len(z_chunks))
    for n, lo in enumerate(conv_chunks):
        sl = slice(lo, lo + CONV_CHUNK)
        p = _dot(xe_ref[...], w_ref[:, inner + lo:inner + lo + CONV_CHUNK])
        if n % per_z == 0:
            zl = z_chunks[n // per_z]
            z_ref[:, zl:zl + FF_CHUNK] = _dot(xe_ref[H0:, :], w_ref[:, zl:zl + FF_CHUNK]).astype(BF16)
        acc = cb_ref[:, sl] + cw_ref[n_sh:n_sh + 1, sl] * p[H0:, :]
        for k in range(n_sh):
            acc = acc + cw_ref[k:k + 1, sl] * p[H0 - n_sh + k:H0 - n_sh + k + tm, :]
        xc_ref[:, sl] = (acc * jax.nn.sigmoid(acc)).astype(BF16)
    for zl in z_chunks[-(-len(conv_chunks) // per_z):]:
        z_ref[:, zl:zl + FF_CHUNK] = _dot(xe_ref[H0:, :], w_ref[:, zl:zl + FF_CHUNK]).astype(BF16)
    dt_ref[...] = _dot(xe_ref[H0:, :], w_ref[:, inner + cc:])


def _ssd_proj(h, g, w, conv_w, conv_b, inner, cc):
    b, s, d = h.shape
    tm = min(TOKEN_TILE, s)
    per_halo = tm // CONV_HALO
    row = lambda n: pl.BlockSpec((None, tm, n), lambda i, j: (i, j, 0))
    return pl.pallas_call(
        _ssd_proj_kernel,
        grid=(b, s // tm),
        in_specs=[row(d),
                  pl.BlockSpec((None, CONV_HALO, d), lambda i, j: (i, jnp.maximum(j * per_halo - 1, 0), 0)),
                  _const_spec((1, d)), _const_spec(w.shape), _const_spec(conv_w.shape), _const_spec((1, cc))],
        out_specs=[row(inner), row(cc), row(LANES)],
        out_shape=[jax.ShapeDtypeStruct((b, s, inner), BF16), jax.ShapeDtypeStruct((b, s, cc), BF16),
                   jax.ShapeDtypeStruct((b, s, LANES), F32)],
        scratch_shapes=[pltpu.VMEM((CONV_HALO + tm, d), BF16)],
        compiler_params=_cparams("parallel", "arbitrary"),
        name="ssd_proj",
    )(h, h, g, w, conv_w, conv_b)


def _ssd_scan_kernel(xc_ref, dt_ref, dtb_ref, alog_ref, dskip_ref, y_ref, state_ref):
    L = xc_ref.shape[0]
    inner = y_ref.shape[1]
    G, N, P = SSD_GROUPS, SSD_STATE, SSD_HEADDIM
    hpg = inner // P // G

    @pl.when(pl.program_id(1) == 0)
    def _():
        state_ref[...] = jnp.zeros(state_ref.shape, F32)

    dt = _softplus(dt_ref[...] + dtb_ref[...])
    da = dt * (jnp.exp(alog_ref[...]) * -LOG2E)
    ti = lax.broadcasted_iota(jnp.int32, (L, L), 0)
    si = lax.broadcasted_iota(jnp.int32, (L, L), 1)
    causal = si <= ti
    tri = jnp.where(causal, 1.0, 0.0).astype(BF16)
    d1, d2, d3 = _split3(da)
    acum = _dot(tri, d1) + _dot(tri, d2) + _dot(tri, d3)
    acum_t = acum.T
    dt_t = dt.T
    dec_end_t = jnp.exp2(acum_t[:, L - 1:L] - acum_t) * dt_t
    e_last = jnp.exp2(acum[L - 1:L, :])
    lane = lax.broadcasted_iota(jnp.int32, (L, 2 * P), 1)
    first = lane < P
    lane1 = lax.broadcasted_iota(jnp.int32, (1, 2 * P), 1)
    first1 = lane1 < P

    for g in range(G):
        bg = xc_ref[:, inner + g * N:inner + (g + 1) * N]
        cg_b = xc_ref[:, inner + G * N + g * N:inner + G * N + (g + 1) * N]
        cg = cg_b.astype(F32)
        cbm = _dot_nt(cg_b, bg)
        bg_t = bg.astype(F32).T
        for pr in range(hpg // 2):
            h0 = g * hpg + 2 * pr
            xsl = slice(h0 * P, (h0 + 2) * P)
            psl = slice(2 * pr * P, (2 * pr + 2) * P)
            x_b = xc_ref[:, xsl]
            st_pair = state_ref[g, :, psl]
            rhs = jnp.concatenate([x_b, st_pair.astype(BF16)], axis=0)
            ys, us = [], []
            for h in (h0, h0 + 1):
                colb = jnp.broadcast_to(acum[:, h:h + 1], (L, L))
                decay = jnp.exp2(jnp.where(causal, colb - acum_t[h:h + 1, :], NEG_BIG))
                w = cbm * decay * dt_t[h:h + 1, :]
                cs = cg * jnp.exp2(colb)
                lhs = jnp.concatenate([w, cs], axis=1).astype(BF16)
                ys.append(_dot(lhs, rhs))
                us.append(_dot((bg_t * dec_end_t[h:h + 1, :]).astype(BF16), x_b))
            y_ref[:, xsl] = (jnp.where(first, ys[0], ys[1])
                             + dskip_ref[:, xsl] * x_b.astype(F32)).astype(y_ref.dtype)
            sd = jnp.where(first1, e_last[:, h0:h0 + 1], e_last[:, h0 + 1:h0 + 2])
            state_ref[g, :, psl] = st_pair * sd + jnp.where(first, us[0], us[1])


def _ssd_scan(xc, dt, dt_bias, a_log, d_skip, inner):
    b, s, cc = xc.shape
    L = min(SSD_L, s)
    G, N = SSD_GROUPS, SSD_STATE
    blk = lambda n: pl.BlockSpec((None, L, n), lambda i, j: (i, j, 0))
    return pl.pallas_call(
        _ssd_scan_kernel,
        grid=(b, s // L),
        in_specs=[blk(cc), blk(LANES), _const_spec((1, LANES)), _const_spec((1, LANES)),
                  _const_spec((1, inner))],
        out_specs=blk(inner),
        out_shape=jax.ShapeDtypeStruct((b, s, inner), BF16),
        scratch_shapes=[pltpu.VMEM((G, N, inner // G), F32)],
        compiler_params=_cparams("parallel", "arbitrary"),
        name="ssd_scan",
    )(xc, dt, dt_bias, a_log, d_skip)


OUT_ROW_BLOCKS = 2


def _ssd_out_kernel(y_ref, z_ref, ng_ref, w_ref, h_ref, g_ref, o_ref):
    tm = y_ref.shape[0]
    rb = tm // OUT_ROW_BLOCKS
    for r0 in range(0, tm, rb):
        rs = slice(r0, r0 + rb)
        z = z_ref[rs, :].astype(F32)
        yg = y_ref[rs, :].astype(F32) * (z * jax.nn.sigmoid(z))
        m = _dot(_rms(yg, ng_ref[...]).astype(BF16), w_ref[...])
        o_ref[rs, :] = h_ref[rs, :] + _rms(m, g_ref[...])


def _ssd_out(y, z, norm_g, w_out, h, g):
    t, d = h.shape
    inner = y.shape[1]
    tm = min(TOKEN_TILE, t)
    row = lambda n: pl.BlockSpec((tm, n), lambda i: (i, 0))
    return pl.pallas_call(
        _ssd_out_kernel,
        grid=(t // tm,),
        in_specs=[row(inner), row(inner), _const_spec((1, inner)), _const_spec(w_out.shape), row(d),
                  _const_spec((1, d))],
        out_specs=row(d),
        out_shape=jax.ShapeDtypeStruct((t, d), F32),
        compiler_params=_cparams("parallel"),
        name="ssd_out",
    )(y, z, norm_g, w_out, h, g)


def _pad_lanes(v):
    return jnp.zeros((1, LANES), F32).at[0, :v.shape[0]].set(v.astype(F32))


def _prep_ssd(w_in, conv_w, conv_b, dt_bias, a_log, d_skip, norm_g, w_out):
    d, n_in = w_in.shape
    n_heads = dt_bias.shape[0]
    inner = n_heads * SSD_HEADDIM
    cc = conv_w.shape[1]
    w = jnp.zeros((d, inner + cc + LANES), BF16).at[:, :n_in].set(w_in.astype(BF16))
    return (w, conv_w.astype(F32), conv_b.astype(F32)[None], _pad_lanes(dt_bias), _pad_lanes(a_log),
            jnp.repeat(d_skip.astype(F32), SSD_HEADDIM)[None], norm_g.astype(F32)[None], w_out.astype(BF16))


def _ssd_layer(h, g_pre, g_post, w, conv_w, conv_b, dt_bias, a_log, d_skip, norm_g, w_out):
    b, s, d = h.shape
    inner = norm_g.shape[-1]
    cc = conv_w.shape[1]
    z, xc, dt = _ssd_proj(h, g_pre, w, conv_w, conv_b, inner, cc)
    y = _ssd_scan(xc, dt, dt_bias, a_log, d_skip, inner)
    return _ssd_out(y.reshape(b * s, inner), z.reshape(b * s, inner), norm_g, w_out, h.reshape(b * s, d),
                    g_post).reshape(b, s, d)


SB_HEADS = 8
NSA_HEADS = 8
NSA_GROUPS = 2
NSA_REP = NSA_HEADS // NSA_GROUPS
CMP_LEN = 32
CMP_STRIDE = 16
SEL_BLOCK = 64
SEL_TOPN = 16
WINDOW = 512
FORCE_BONUS = 1e4
NSA_QB = 128
SB_W = SB_HEADS * HEAD_DIM
NSA_QW = NSA_HEADS * HEAD_DIM
NSA_KVW = NSA_GROUPS * HEAD_DIM
ATT_SCALE = HEAD_DIM ** -0.5 * LOG2E

_C_QSB, _C_KSB, _C_VSB = 0, SB_W, 2 * SB_W
_C_QN = 3 * SB_W
_C_QNP = _C_QN + NSA_QW
_C_K3 = _C_QNP + NSA_QW
_C_K3P = _C_K3 + 3 * NSA_KVW
_C_V3 = _C_K3P + 3 * NSA_KVW
_C_GATE = _C_V3 + 3 * NSA_KVW
_C_END = _C_GATE + LANES


def _hyb_proj_kernel(h_ref, g_ref, pos_ref, inv_ref, sgn_ref, w_ref,
                     qsb_ref, ksb_ref, vsb_ref, qn_ref, kcmp_ref, vcmp_ref,
                     ksel_ref, vsel_ref, kwin_ref, vwin_ref, gate_ref):
    xn = _rms(h_ref[...], g_ref[...]).astype(BF16)
    tm = xn.shape[0]
    hd = HEAD_DIM

    def proj(lo, n):
        return _dot(xn, w_ref[:, lo:lo + n])

    for ref, lo, scale in ((qsb_ref, _C_QSB, ATT_SCALE), (ksb_ref, _C_KSB, 1.0), (vsb_ref, _C_VSB, 1.0)):
        p = proj(lo, SB_W) * scale
        for a in range(SB_HEADS):
            ref[a] = p[:, a * hd:(a + 1) * hd].astype(BF16)

    ang = pos_ref[...] * inv_ref[...]
    cos = jnp.cos(ang)
    sin = jnp.sin(ang) * sgn_ref[...]

    def rot(lo, lop, n):
        reps = n // LANES
        return (proj(lo, n) * jnp.concatenate([cos] * reps, axis=1)
                + proj(lop, n) * jnp.concatenate([sin] * reps, axis=1))

    qn = rot(_C_QN, _C_QNP, NSA_QW) * ATT_SCALE
    lane = lax.broadcasted_iota(jnp.int32, (tm, LANES), 1)
    low = lane < hd
    for a in range(NSA_HEADS):
        grp = a // NSA_REP
        piece = qn[:, (a // 2) * LANES:(a // 2 + 1) * LANES]
        if (a % 2) != grp:
            piece = pltpu.roll(piece, hd, 1)
        keep = low if grp == 0 else jnp.logical_not(low)
        qn_ref[a] = jnp.where(keep, piece, 0.0).astype(BF16)

    k3 = rot(_C_K3, _C_K3P, 3 * NSA_KVW)
    v3 = proj(_C_V3, 3 * NSA_KVW)
    for grp in range(NSA_GROUPS):
        kcmp_ref[grp] = k3[:, grp * hd:(grp + 1) * hd].astype(BF16)
        vcmp_ref[grp] = v3[:, grp * hd:(grp + 1) * hd].astype(BF16)
    ksel_ref[...] = k3[:, NSA_KVW:2 * NSA_KVW].astype(BF16)
    vsel_ref[...] = v3[:, NSA_KVW:2 * NSA_KVW].astype(BF16)
    kwin_ref[...] = k3[:, 2 * NSA_KVW:].astype(BF16)
    vwin_ref[...] = v3[:, 2 * NSA_KVW:].astype(BF16)
    gate_ref[...] = jax.nn.sigmoid(proj(_C_GATE, LANES))


def _hyb_proj(h, g, posf, inv, sgn, w):
    b, s, d = h.shape
    tm = min(TOKEN_TILE, s)
    heads = lambda n, w_: pl.BlockSpec((None, n, tm, w_), lambda i, j: (i, 0, j, 0))
    rows = lambda w_: pl.BlockSpec((None, tm, w_), lambda i, j: (i, j, 0))
    hshape = lambda n, w_: jax.ShapeDtypeStruct((b, n, s, w_), BF16)
    rshape = lambda w_, dt: jax.ShapeDtypeStruct((b, s, w_), dt)
    return pl.pallas_call(
        _hyb_proj_kernel,
        grid=(b, s // tm),
        in_specs=[rows(d), _const_spec((1, d)), rows(1), _const_spec((1, LANES)), _const_spec((1, LANES)),
                  _const_spec(w.shape)],
        out_specs=[heads(SB_HEADS, HEAD_DIM)] * 3 + [heads(NSA_HEADS, LANES)]
        + [heads(NSA_GROUPS, HEAD_DIM)] * 2 + [rows(LANES)] * 5,
        out_shape=[hshape(SB_HEADS, HEAD_DIM)] * 3 + [hshape(NSA_HEADS, LANES)]
        + [hshape(NSA_GROUPS, HEAD_DIM)] * 2 + [rshape(LANES, BF16)] * 4 + [rshape(LANES, F32)],
        compiler_params=_cparams("parallel", "parallel"),
        name="hyb_proj",
    )(h, g, posf, inv, sgn, w)


SB_TILE = 256
SB_SUB = 128
SB_DEAD_LOG2 = 160.0
SB_ROW_BANDS = 4


def _sb_blocks(heads, later2):
    jobs = [(h, off, k, v, mask) for h, (_, _, subs) in enumerate(heads) for off, k, v, mask in subs]
    ts = [_dot_nt(heads[h][0][off:], k) for h, off, k, _, _ in jobs]
    sps, hls = [], []
    for t, (_, _, _, _, mask) in zip(ts, jobs):
        neg_abs = lax.bitcast_convert_type(lax.bitcast_convert_type(t, jnp.uint32) | jnp.uint32(0x80000000),
                                           F32)
        sp = jnp.maximum(t, 0.0) + jnp.log2(1.0 + jnp.exp2(neg_abs))
        if mask is not None:
            sp = jnp.where(mask, sp, 0.0)
        hi = sp.astype(BF16)
        lo = (sp - hi.astype(F32)).astype(BF16)
        sps.append(sp)
        hls.append(jnp.concatenate([hi, lo], axis=1))
    css = [_dot(hl, later2) for hl in hls]
    carries = [carry for _, carry, _ in heads]
    ws = []
    for t, sp, cs, (h, off, _, _, mask) in zip(ts, sps, css, jobs):
        carry = carries[h]
        w = jnp.exp2(t - sp - cs - carry[off:])
        if mask is not None:
            w = jnp.where(mask, w, 0.0)
        ws.append(w.astype(BF16))
        grown = carry[off:] + cs[:, 0:1] + sp[:, 0:1]
        carries[h] = grown if off == 0 else jnp.concatenate([carry[:off], grown], axis=0)
    outs = [[] for _ in heads]
    for w, (h, off, _, v, _) in zip(ws, jobs):
        outs[h].append((off, _dot(w, v)))
    return list(zip(outs, carries))


def _sb_kernel(q_ref, kd_ref, vd_ref, k_hbm, v_hbm, o_ref, acc_ref, carry_ref, kbuf, vbuf, sem):
    bi = pl.program_id(0)
    i = pl.program_id(1)
    n_heads, tq, hd = q_ref.shape
    sub = min(SB_SUB, tq)
    n_sub = tq // sub
    n_prev = i

    def copies(n, slot):
        start = pl.multiple_of((n_prev - 1 - n) * tq, tq)
        return (pltpu.make_async_copy(k_hbm.at[bi, :, pl.ds(start, tq), :], kbuf.at[slot], sem.at[0, slot]),
                pltpu.make_async_copy(v_hbm.at[bi, :, pl.ds(start, tq), :], vbuf.at[slot], sem.at[1, slot]))

    def start_fetch(n, slot):
        for c in copies(n, slot):
            c.start()

    def wait_fetch(n, slot):
        for c in copies(n, slot):
            c.wait()

    @pl.when(n_prev > 0)
    def _():
        start_fetch(0, 0)

    kr = lax.broadcasted_iota(jnp.int32, (2 * sub, sub), 0) % sub
    kc = lax.broadcasted_iota(jnp.int32, (2 * sub, sub), 1)
    later2 = jnp.where(kr > kc, 1.0, 0.0).astype(BF16)

    acc_ref[...] = jnp.zeros(acc_ref.shape, F32)
    carry_ref[...] = jnp.zeros(carry_ref.shape, F32)

    def apply(results, rows):
        for a, (pvs, carry) in enumerate(results):
            for off, pv in pvs:
                acc_ref[a, off:rows, :] += pv
            carry_ref[a, :rows, :] = carry

    def diag_subs(a):
        subs = []
        for sb in reversed(range(n_sub)):
            r0 = sb * sub
            mask = (lax.broadcasted_iota(jnp.int32, (tq - r0, sub), 1)
                    < lax.broadcasted_iota(jnp.int32, (tq - r0, sub), 0))
            subs.append((r0, kd_ref[a, r0:r0 + sub, :], vd_ref[a, r0:r0 + sub, :], mask))
        return subs

    apply(_sb_blocks([(q_ref[a], carry_ref[a], diag_subs(a)) for a in range(n_heads)], later2), tq)

    band = tq // SB_ROW_BANDS

    def live_bands():
        least = carry_ref[0]
        for a in range(1, n_heads):
            least = jnp.minimum(least, carry_ref[a])
        band_no = lax.broadcasted_iota(jnp.int32, (tq, 1), 0) // band + 1
        return jnp.max(jnp.where(least < SB_DEAD_LOG2, band_no, 0))

    def body(state):
        n, n_live = state
        slot = n % 2
        wait_fetch(n, slot)

        @pl.when(n + 1 < n_prev)
        def _():
            start_fetch(n + 1, 1 - slot)

        for r in range(1, SB_ROW_BANDS + 1):
            @pl.when(n_live == r)
            def _(rows=r * band):
                def subs(a):
                    return [(0, kbuf[slot, a, sb * sub:(sb + 1) * sub, :], vbuf[slot, a, sb * sub:(sb + 1) * sub, :],
                             None) for sb in reversed(range(n_sub))]

                apply(_sb_blocks([(q_ref[a, :rows, :], carry_ref[a, :rows, :], subs(a))
                                  for a in range(n_heads)], later2), rows)
        return n + 1, live_bands()

    n_end, _ = lax.while_loop(lambda st: (st[0] < n_prev) & (st[1] > 0), body, (jnp.int32(0), live_bands()))

    @pl.when(n_end < n_prev)
    def _():
        wait_fetch(n_end, n_end % 2)

    for a in range(n_heads):
        o_ref[:, a * hd:(a + 1) * hd] = acc_ref[a].astype(o_ref.dtype)


def _sb_attn(q, k, v):
    b, n_heads, s, hd = q.shape
    t = min(SB_TILE, s)
    tile = pl.BlockSpec((None, n_heads, t, hd), lambda bi, i: (bi, 0, i, 0))
    hbm = pl.BlockSpec(memory_space=pl.ANY)
    return pl.pallas_call(
        _sb_kernel,
        grid=(b, s // t),
        in_specs=[tile, tile, tile, hbm, hbm],
        out_specs=pl.BlockSpec((None, t, n_heads * hd), lambda bi, i: (bi, i, 0)),
        out_shape=jax.ShapeDtypeStruct((b, s, n_heads * hd), BF16),
        scratch_shapes=[pltpu.VMEM((n_heads, t, hd), F32), pltpu.VMEM((n_heads, t, 1), F32),
                        pltpu.VMEM((2, n_heads, t, hd), BF16), pltpu.VMEM((2, n_heads, t, hd), BF16),
                        pltpu.SemaphoreType.DMA((2, 2))],
        compiler_params=_cparams("parallel", "arbitrary"),
        name="sb_attn",
    )(q, k, v, k, v)


def _compress_kernel(t_ref, wab_ref, pe_ref, w1_ref, w2_ref, o_ref):
    n_grp, nch, _ = t_ref.shape
    hid = w2_ref.shape[0]
    pe8 = jnp.broadcast_to(pe_ref[...], (8, pe_ref.shape[1])).astype(BF16)
    bias = _dot(pe8, w1_ref[...])[0:1, :]
    for grp in range(n_grp):
        ab = _dot(t_ref[grp], wab_ref[...])
        nxt = pltpu.roll(ab[:, hid:], nch - 1, 0)
        hcur = ab[:, :hid] + nxt + bias
        act = (hcur * jax.nn.sigmoid(hcur)).astype(BF16)
        o_ref[:, grp * HEAD_DIM:(grp + 1) * HEAD_DIM] = _dot(act, w2_ref[...]).astype(o_ref.dtype)


def _compress(t, wab, pe, w1, w2):
    b, n_grp, nch, kw = t.shape
    return pl.pallas_call(
        _compress_kernel,
        grid=(b,),
        in_specs=[pl.BlockSpec((None, n_grp, nch, kw), lambda i: (i, 0, 0, 0)), _const_spec(wab.shape),
                  _const_spec(pe.shape), _const_spec(w1.shape), _const_spec(w2.shape)],
        out_specs=pl.BlockSpec((None, nch, n_grp * HEAD_DIM), lambda i: (i, 0, 0)),
        out_shape=jax.ShapeDtypeStruct((b, nch, n_grp * HEAD_DIM), BF16),
        compiler_params=_cparams("parallel"),
        name="nsa_compress",
    )(t, wab, pe, w1, w2)


def _masked_softmax_rows(s, mask):
    sm = jnp.where(mask, s, NEG_BIG)
    m = jnp.max(sm, axis=-1, keepdims=True)
    e = jnp.where(mask, jnp.exp2(sm - m), 0.0)
    den = jnp.sum(e, axis=-1, keepdims=True)
    return e / jnp.maximum(den, 1e-30)


def _softmax_num(s, bias, groups):
    rows, n = s.shape
    sm = (s.reshape(groups, rows // groups, n) + bias[None]).reshape(rows, n)
    e = jnp.exp2(sm - jnp.max(sm, axis=-1, keepdims=True))
    return e.astype(BF16), jnp.sum(e, axis=-1, keepdims=True)


RANK_UNROLL = 8


def _select_kernel(qf_ref, kc_ref, o_ref):
    n_heads, nsub, _ = qf_ref.shape
    nch = kc_ref.shape[0]
    q = qf_ref[...].reshape(n_heads * nsub, LANES)
    s = _dot_nt(q, kc_ref[...])
    sub_id = lax.broadcasted_iota(jnp.int32, (n_heads * nsub, nch), 0) % nsub
    c_id = lax.broadcasted_iota(jnp.int32, (n_heads * nsub, nch), 1)
    p = _masked_softmax_rows(s, c_id * CMP_STRIDE + (CMP_LEN - 1) <= sub_id * SEL_BLOCK)
    rows = NSA_GROUPS * nsub
    psum = jnp.concatenate(
        [sum(p[(grp * NSA_REP + r) * nsub:(grp * NSA_REP + r + 1) * nsub] for r in range(NSA_REP))
         for grp in range(NSA_GROUPS)], axis=0)
    oc = lax.broadcasted_iota(jnp.int32, (nch, LANES), 0) * CMP_STRIDE
    oj = lax.broadcasted_iota(jnp.int32, (nch, LANES), 1) * SEL_BLOCK
    overlap = jnp.where((oc < oj + SEL_BLOCK) & (oc + (CMP_LEN - 1) >= oj), 1.0, 0.0).astype(BF16)
    p1, p2, p3 = _split3(psum)
    imp = _dot(p1, overlap) + _dot(p2, overlap) + _dot(p3, overlap)
    jl = lax.broadcasted_iota(jnp.int32, (rows, LANES), 1)
    cur = lax.broadcasted_iota(jnp.int32, (rows, LANES), 0) % nsub
    forced = (jl == 0) | (jl == cur) | (jl == cur - 1)
    score = jnp.where(jl <= cur, imp + jnp.where(forced, FORCE_BONUS, 0.0), -1.0)

    def body(d, rank):
        other = pltpu.roll(score, d, 1)
        ahead = (other > score) | ((other == score) & (jl >= d))
        return rank + jnp.where(ahead, 1.0, 0.0)

    rank = lax.fori_loop(1, LANES, body, jnp.zeros((rows, LANES), F32), unroll=RANK_UNROLL)
    jf = jl.astype(F32)
    out = jnp.zeros((rows, LANES), F32)
    for r in range(SEL_TOPN):
        col = jnp.sum(jnp.where(rank == float(r), jf, 0.0), axis=-1, keepdims=True)
        out = jnp.where(jl == r, col, out)
    o_ref[...] = out.astype(jnp.int32)


def _nsa_select(qf, kc):
    b, n_heads, nsub, _ = qf.shape
    nch = kc.shape[1]
    assert SEL_TOPN <= nsub <= LANES
    return pl.pallas_call(
        _select_kernel,
        grid=(b,),
        in_specs=[pl.BlockSpec((None, n_heads, nsub, LANES), lambda i: (i, 0, 0, 0)),
                  pl.BlockSpec((None, nch, LANES), lambda i: (i, 0, 0))],
        out_specs=pl.BlockSpec((None, NSA_GROUPS * nsub, LANES), lambda i: (i, 0, 0)),
        out_shape=jax.ShapeDtypeStruct((b, NSA_GROUPS * nsub, LANES), jnp.int32),
        compiler_params=_cparams("parallel"),
        name="nsa_select",
    )(qf, kc)


def _nsa_kernel(sel_ref, qn_ref, kc_ref, vc_ref, ksel_ref, vsel_ref, kwin_ref, vwin_ref, gate_ref,
                o_ref, kg_ref, vg_ref):
    bi = pl.program_id(0)
    i = pl.program_id(1)
    n_heads, qb, _ = qn_ref.shape
    s_len = ksel_ref.shape[0]
    nsub_total = s_len // SEL_BLOCK
    nch = kc_ref.shape[0]
    hd = HEAD_DIM
    q = qn_ref[...].reshape(n_heads * qb, LANES)
    qpos = i * qb + lax.broadcasted_iota(jnp.int32, (qb, 1), 0)

    def bias_of(mask):
        return jnp.where(mask, 0.0, NEG_BIG)

    nsel = SEL_TOPN * SEL_BLOCK
    lane_blk = lax.broadcasted_iota(jnp.int32, (1, nsel), 1) // SEL_BLOCK
    lane_off = lax.broadcasted_iota(jnp.int32, (1, nsel), 1) % SEL_BLOCK
    n_sub = qb // SEL_BLOCK
    sel_jobs = []
    for grp in range(NSA_GROUPS):
        for n in range(n_sub):
            slot = grp * n_sub + n
            base = ((bi * NSA_GROUPS + grp) * nsub_total + i * n_sub + n) * SEL_TOPN
            tok = lane_off
            for t in range(SEL_TOPN):
                idx = sel_ref[base + t]
                off = pl.multiple_of(idx * SEL_BLOCK, SEL_BLOCK)
                kg_ref[slot, t * SEL_BLOCK:(t + 1) * SEL_BLOCK, :] = ksel_ref[pl.ds(off, SEL_BLOCK), :]
                vg_ref[slot, t * SEL_BLOCK:(t + 1) * SEL_BLOCK, :] = vsel_ref[pl.ds(off, SEL_BLOCK), :]
                tok = tok + jnp.where(lane_blk == t, idx * SEL_BLOCK, 0)
            qs = qn_ref[grp * NSA_REP:(grp + 1) * NSA_REP, n * SEL_BLOCK:(n + 1) * SEL_BLOCK, :]
            qp = i * qb + n * SEL_BLOCK + lax.broadcasted_iota(jnp.int32, (SEL_BLOCK, 1), 0)
            sel_jobs.append((slot, qs.reshape(NSA_REP * SEL_BLOCK, LANES), bias_of(tok <= qp)))

    c_end = lax.broadcasted_iota(jnp.int32, (1, nch), 1) * CMP_STRIDE + (CMP_LEN - 1)
    wlen = WINDOW + qb
    start = pl.multiple_of(jnp.maximum(i * qb - WINDOW, 0), qb)
    delta = qpos - (start + lax.broadcasted_iota(jnp.int32, (1, wlen), 1))

    s_c = _dot_nt(q, kc_ref[...])
    s_w = _dot_nt(q, kwin_ref[pl.ds(start, wlen), :])
    s_s = [_dot_nt(qs, kg_ref[slot]) for slot, qs, _ in sel_jobs]

    e_c, den_c = _softmax_num(s_c, bias_of(c_end <= qpos), n_heads)
    e_w, den_w = _softmax_num(s_w, bias_of((delta >= 0) & (delta < WINDOW)), n_heads)
    num_s = [_softmax_num(s, bias, NSA_REP) for s, (_, _, bias) in zip(s_s, sel_jobs)]

    any_c = jnp.where(qpos >= CMP_LEN - 1, 1.0, 0.0)
    o_c = _dot(e_c, vc_ref[...]) / den_c
    o_c = (o_c.reshape(n_heads, qb, LANES) * any_c[None]).reshape(n_heads * qb, LANES)
    o_w = _dot(e_w, vwin_ref[pl.ds(start, wlen), :]) / den_w
    o_sel = [_dot(e, vg_ref[slot]) / den for (e, den), (slot, _, _) in zip(num_s, sel_jobs)]
    o_s = [[o_sel[grp * n_sub + n] for n in range(n_sub)] for grp in range(NSA_GROUPS)]


    lane = lax.broadcasted_iota(jnp.int32, (qb, LANES), 1)
    low = lane < hd
    gates = gate_ref[...]
    mixed = []
    for a in range(n_heads):
        grp, r = a // NSA_REP, a % NSA_REP
        rs = slice(a * qb, (a + 1) * qb)
        sel_rows = jnp.concatenate([o_s[grp][n][r * SEL_BLOCK:(r + 1) * SEL_BLOCK] for n in range(n_sub)],
                                   axis=0)
        mixed.append(gates[:, 3 * a:3 * a + 1] * o_c[rs] + gates[:, 3 * a + 1:3 * a + 2] * sel_rows
                     + gates[:, 3 * a + 2:3 * a + 3] * o_w[rs])
    for pr in range(n_heads // 2):
        grp = (2 * pr) // NSA_REP
        left, right = mixed[2 * pr], mixed[2 * pr + 1]
        if grp == 0:
            right = pltpu.roll(right, hd, 1)
        else:
            left = pltpu.roll(left, hd, 1)
        o_ref[:, pr * LANES:(pr + 1) * LANES] = jnp.where(low, left, right).astype(o_ref.dtype)


def _nsa_attn(sel, qn, kc, vc, ksel, vsel, kwin, vwin, gates):
    b, n_heads, s, _ = qn.shape
    nch = kc.shape[1]
    qb = min(NSA_QB, s)
    assert s >= WINDOW + qb
    full = lambda n: pl.BlockSpec((None, n, LANES), lambda bi, i, sel: (bi, 0, 0))
    grid_spec = pltpu.PrefetchScalarGridSpec(
        num_scalar_prefetch=1,
        grid=(b, s // qb),
        in_specs=[pl.BlockSpec((None, n_heads, qb, LANES), lambda bi, i, sel: (bi, 0, i, 0)),
                  full(nch), full(nch), full(s), full(s), full(s), full(s),
                  pl.BlockSpec((None, qb, LANES), lambda bi, i, sel: (bi, i, 0))],
        out_specs=pl.BlockSpec((None, qb, n_heads * HEAD_DIM), lambda bi, i, sel: (bi, i, 0)),
        scratch_shapes=[pltpu.VMEM((NSA_GROUPS * (qb // SEL_BLOCK), SEL_TOPN * SEL_BLOCK, LANES), BF16)] * 2,
    )
    return pl.pallas_call(
        _nsa_kernel,
        grid_spec=grid_spec,
        out_shape=jax.ShapeDtypeStruct((b, s, n_heads * HEAD_DIM), BF16),
        compiler_params=_cparams("parallel", "arbitrary"),
        name="nsa_attn",
    )(sel, qn, kc, vc, ksel, vsel, kwin, vwin, gates)


def _prep_hybrid(w_in, w_out, pe_k, w1_k, w2_k, pe_v, w1_v, w2_v):
    d = w_in.shape[0]
    hd = HEAD_DIM
    col = np.arange(w_in.shape[1])
    o_qn = 3 * SB_W
    o_kv = o_qn + NSA_QW
    kv = lambda n: col[o_kv + n * NSA_KVW:o_kv + (n + 1) * NSA_KVW]
    partner = lambda c: (c // hd) * hd + (c % hd + hd // 2) % hd
    qn_c = col[o_qn:o_qn + NSA_QW]
    k3_c = np.concatenate([kv(0), kv(2), kv(4)])
    v3_c = np.concatenate([kv(1), kv(3), kv(5)])
    gate_c = col[o_kv + 6 * NSA_KVW:]
    order = np.concatenate([col[:o_qn], qn_c, partner(qn_c), k3_c, partner(k3_c), v3_c, gate_c])
    w = jnp.zeros((d, _C_END), BF16).at[:, :order.shape[0]].set(w_in[:, order].astype(BF16))
    lanes = np.arange(LANES)
    inv = (ROPE_THETA ** (-(lanes % ROPE_HALF).astype(np.float32) / ROPE_HALF)).astype(np.float32)[None]
    sgn = np.where(lanes % hd < ROPE_HALF, -1.0, 1.0).astype(np.float32)[None]

    def cmp_w(pe, w1, w2):
        half = w1.shape[0] // 2
        wab = jnp.concatenate([w1[:half], w1[half:]], axis=1).astype(BF16)
        return wab, pe.reshape(1, -1).astype(F32), w1.astype(BF16), w2.astype(BF16)

    return (w, jnp.asarray(inv), jnp.asarray(sgn), w_out[:SB_W].astype(BF16), w_out[SB_W:].astype(BF16),
            cmp_w(pe_k, w1_k, w2_k), cmp_w(pe_v, w1_v, w2_v))


def _hybrid_layer(h, positions, g_pre, g_post, w, inv, sgn, wo_sb, wo_nsa, cmp_k, cmp_v):
    b, s, d = h.shape
    posf = positions.astype(F32)[..., None]
    (qsb, ksb, vsb, qn, kcmp, vcmp, ksel, vsel, kwin, vwin, gates) = _hyb_proj(h, g_pre, posf, inv, sgn, w)
    o_sb = _sb_attn(qsb, ksb, vsb)
    half = CMP_LEN // 2
    chunks = lambda t: t.reshape(b, NSA_GROUPS, s // half, half * HEAD_DIM)
    kc = _compress(chunks(kcmp), *cmp_k)
    vc = _compress(chunks(vcmp), *cmp_v)
    sel = _nsa_select(qn[:, :, ::SEL_BLOCK, :], kc)
    sel = sel[:, :, :SEL_TOPN].reshape(-1)
    o_nsa = _nsa_attn(sel, qn, kc, vc, ksel, vsel, kwin, vwin, gates)
    hf = h.reshape(b * s, d)
    return _outproj([o_sb.reshape(b * s, SB_W), o_nsa.reshape(b * s, NSA_QW)], [wo_sb, wo_nsa], hf,
                    g_post).reshape(b, s, d)


CROSS_HEADS = 4


def kernel(x, mem, positions, norm_g, ffn1_w_gu, ffn1_w_down, ffn2_w_gu, ffn2_w_down, cross_wq, cross_wkv,
           cross_wo, hyb_w_in, hyb_w_out, cmp_pe_k, cmp_w1_k, cmp_w2_k, cmp_pe_v, cmp_w1_v, cmp_w2_v,
           ssd_w_in, ssd_conv_w, ssd_conv_b, ssd_dt_bias, ssd_A_log, ssd_D, ssd_norm_g, ssd_w_out):
    b, s, d = x.shape
    depth = norm_g.shape[0]
    h = x
    for i in range(depth):
        g = norm_g[i].astype(F32)[:, None, :]
        j = i // 2
        h = _ffn(h.reshape(b * s, d), g[0], g[1], ffn1_w_gu[i].astype(BF16),
                 ffn1_w_down[i].astype(BF16)).reshape(b, s, d)
        if i % 2 == 0:
            h = _hybrid_layer(h, positions, g[2], g[3],
                              *_prep_hybrid(hyb_w_in[j], hyb_w_out[j], cmp_pe_k[j], cmp_w1_k[j], cmp_w2_k[j],
                                            cmp_pe_v[j], cmp_w1_v[j], cmp_w2_v[j]))
        else:
            h = _ssd_layer(h, g[2], g[3], *_prep_ssd(ssd_w_in[j], ssd_conv_w[j], ssd_conv_b[j], ssd_dt_bias[j],
                                                     ssd_A_log[j], ssd_D[j], ssd_norm_g[j], ssd_w_out[j]))
        k, v = _memkv(mem, g[6], cross_wkv[i].astype(BF16))
        h = _cross(h, g[4], g[5], cross_wq[i].astype(BF16), k, v, cross_wo[i].astype(BF16), CROSS_HEADS)
        h = _ffn(h.reshape(b * s, d), g[7], g[8], ffn2_w_gu[i].astype(BF16),
                 ffn2_w_down[i].astype(BF16)).reshape(b, s, d)
    return h
```

```python
import functools
import math

import jax
import jax.numpy as jnp
import numpy as np
from jax import lax
from jax.experimental import pallas as pl
from jax.experimental.pallas import tpu as pltpu

F32 = jnp.float32
BF16 = jnp.bfloat16

RMS_EPS = 1e-6
LOG2E = 1.4426950408889634
ROPE_THETA = 10000.0
HEAD_DIM = 64
ROPE_HALF = HEAD_DIM // 2

V7X_VMEM_BYTES = 64 * 1024 * 1024
VMEM_LIMIT = (V7X_VMEM_BYTES * 3) // 4
LANES = 128

TOKEN_TILE = 512
FF_CHUNK = 512


def _cparams(*sem):
    return pltpu.CompilerParams(dimension_semantics=sem, vmem_limit_bytes=VMEM_LIMIT)


def _rms(x, g):
    ms = jnp.mean(x * x, axis=-1, keepdims=True)
    return x * lax.rsqrt(ms + RMS_EPS) * g


def _dot(a, b):
    return jnp.dot(a, b, preferred_element_type=F32)


def _dot_nt(a, b):
    return lax.dot_general(a, b, (((1,), (1,)), ((), ())), preferred_element_type=F32)


def _const_spec(shape):
    nd = len(shape)
    return pl.BlockSpec(shape, lambda *_: (0,) * nd)


FFN_ROW_BLOCKS = 1


def _ffn_kernel(h_ref, gpre_ref, gpost_ref, wgu_ref, wd_ref, o_ref):
    d_ff = wd_ref.shape[0]
    tm = h_ref.shape[0]
    rb = tm // FFN_ROW_BLOCKS
    blocks = [slice(r0, r0 + rb) for r0 in range(0, tm, rb)]
    xns = [_rms(h_ref[rs, :], gpre_ref[...]).astype(BF16) for rs in blocks]
    for rs, xn in zip(blocks, xns):
        acc = jnp.zeros((rb, h_ref.shape[1]), F32)
        for c in range(d_ff // FF_CHUNK):
            lo = c * FF_CHUNK
            gate = _dot(xn, wgu_ref[:, lo:lo + FF_CHUNK])
            up = _dot(xn, wgu_ref[:, d_ff + lo:d_ff + lo + FF_CHUNK])
            act = (gate * jax.nn.sigmoid(gate) * up).astype(BF16)
            acc = acc + _dot(act, wd_ref[lo:lo + FF_CHUNK, :])
        o_ref[rs, :] = h_ref[rs, :] + 0.5 * _rms(acc, gpost_ref[...])


def _ffn(h, g_pre, g_post, w_gu, w_down):
    t, d = h.shape
    d_ff = w_down.shape[0]
    tm = min(TOKEN_TILE * FFN_ROW_BLOCKS, t)
    return pl.pallas_call(
        _ffn_kernel,
        grid=(t // tm,),
        in_specs=[
            pl.BlockSpec((tm, d), lambda i: (i, 0)),
            _const_spec((1, d)),
            _const_spec((1, d)),
            _const_spec((d, 2 * d_ff)),
            _const_spec((d_ff, d)),
        ],
        out_specs=pl.BlockSpec((tm, d), lambda i: (i, 0)),
        out_shape=jax.ShapeDtypeStruct((t, d), F32),
        compiler_params=_cparams("parallel"),
        name="ffn",
    )(h, g_pre, g_post, w_gu, w_down)


def _memkv_kernel(mem_ref, g_ref, wkv_ref, k_ref, v_ref):
    d = mem_ref.shape[-1]
    mn = _rms(mem_ref[...], g_ref[...]).astype(BF16)
    kv = _dot(mn, wkv_ref[...])
    k_ref[...] = kv[:, :d].astype(BF16)
    v_ref[...] = kv[:, d:].astype(BF16)


def _memkv(mem, g_mem, wkv):
    b, m, d = mem.shape
    return pl.pallas_call(
        _memkv_kernel,
        grid=(b,),
        in_specs=[
            pl.BlockSpec((None, m, d), lambda i: (i, 0, 0)),
            _const_spec((1, d)),
            _const_spec((d, 2 * d)),
        ],
        out_specs=[pl.BlockSpec((None, m, d), lambda i: (i, 0, 0))] * 2,
        out_shape=[jax.ShapeDtypeStruct((b, m, d), BF16)] * 2,
        compiler_params=_cparams("parallel"),
        name="memkv",
    )(mem, g_mem, wkv)


def _cross_kernel(n_heads, h_ref, gpre_ref, gpost_ref, wq_ref, k_ref, v_ref, wo_ref, o_ref):
    h = h_ref[...]
    d = h.shape[-1]
    hd = d // n_heads
    hn = _rms(h, gpre_ref[...]).astype(BF16)
    q = (_dot(hn, wq_ref[...]) * (hd ** -0.5 * LOG2E)).astype(BF16)
    sls = [slice(a * hd, (a + 1) * hd) for a in range(n_heads)]
    ss = [_dot_nt(q[:, sl], k_ref[:, sl]) for sl in sls]
    es = [jnp.exp2(s - jnp.max(s, axis=-1, keepdims=True)) for s in ss]
    os_ = [(_dot(e.astype(BF16), v_ref[:, sl]) / jnp.sum(e, axis=-1, keepdims=True)).astype(BF16)
           for e, sl in zip(es, sls)]
    c = _dot(jnp.concatenate(os_, axis=1), wo_ref[...])
    o_ref[...] = h + _rms(c, gpost_ref[...])


def _cross(h, g_pre, g_post, wq, k, v, wo, n_heads):
    b, s, d = h.shape
    m = k.shape[1]
    tm = min(TOKEN_TILE, s)
    return pl.pallas_call(
        functools.partial(_cross_kernel, n_heads),
        grid=(b, s // tm),
        in_specs=[
            pl.BlockSpec((None, tm, d), lambda i, j: (i, j, 0)),
            _const_spec((1, d)),
            _const_spec((1, d)),
            _const_spec((d, d)),
            pl.BlockSpec((None, m, d), lambda i, j: (i, 0, 0)),
            pl.BlockSpec((None, m, d), lambda i, j: (i, 0, 0)),
            _const_spec((d, d)),
        ],
        out_specs=pl.BlockSpec((None, tm, d), lambda i, j: (i, j, 0)),
        out_shape=jax.ShapeDtypeStruct((b, s, d), F32),
        compiler_params=_cparams("parallel", "parallel"),
        name="cross",
    )(h, g_pre, g_post, wq, k, v, wo)


def _outproj_kernel(n_in, *refs):
    a_refs = refs[:n_in]
    w_refs = refs[n_in:2 * n_in]
    h_ref, g_ref, o_ref = refs[2 * n_in:]
    m = _dot(a_refs[0][...], w_refs[0][...])
    for a_ref, w_ref in zip(a_refs[1:], w_refs[1:]):
        m = m + _dot(a_ref[...], w_ref[...])
    o_ref[...] = h_ref[...] + _rms(m, g_ref[...])


def _outproj(acts, weights, h, g):
    t, d = h.shape
    tm = min(TOKEN_TILE, t)
    n_in = len(acts)
    return pl.pallas_call(
        functools.partial(_outproj_kernel, n_in),
        grid=(t // tm,),
        in_specs=(
            [pl.BlockSpec((tm, a.shape[1]), lambda i: (i, 0)) for a in acts]
            + [_const_spec(w.shape) for w in weights]
            + [pl.BlockSpec((tm, d), lambda i: (i, 0)), _const_spec((1, d))]
        ),
        out_specs=pl.BlockSpec((tm, d), lambda i: (i, 0)),
        out_shape=jax.ShapeDtypeStruct((t, d), F32),
        compiler_params=_cparams("parallel"),
        name="outproj",
    )(*acts, *weights, h, g)


SSD_HEADDIM = 64
SSD_GROUPS = 4
SSD_STATE = 128
SSD_CONV = 4
SSD_L = 128
CONV_HALO = 16
NEG_BIG = -1e30


def _split3(x):
    a = x.astype(BF16)
    r = x - a.astype(F32)
    b = r.astype(BF16)
    c = (r - b.astype(F32)).astype(BF16)
    return a, b, c


def _softplus(x):
    return jnp.maximum(x, 0.0) + jnp.log(1.0 + jnp.exp(-jnp.abs(x)))


CONV_CHUNK = 256


def _ssd_proj_kernel(h_ref, halo_ref, g_ref, w_ref, cw_ref, cb_ref, z_ref, xc_ref, dt_ref, xe_ref):
    tm = h_ref.shape[0]
    inner = z_ref.shape[-1]
    cc = xc_ref.shape[-1]
    H0 = CONV_HALO
    n_sh = SSD_CONV - 1
    xe_ref[0:H0, :] = jnp.where(pl.program_id(1) > 0, _rms(halo_ref[...], g_ref[...]), 0.0).astype(BF16)
    xe_ref[H0:, :] = _rms(h_ref[...], g_ref[...]).astype(BF16)
    z_chunks = list(range(0, inner, FF_CHUNK))
    conv_chunks = list(range(0, cc, CONV_CHUNK))
    per_z = -(-len(conv_chunks) // len(z_chunks))
    for n, lo in enumerate(conv_chunks):
        sl = slice(lo, lo + CONV_CHUNK)
        p = _dot(xe_ref[...], w_ref[:, inner + lo:inner + lo + CONV_CHUNK])
        if n % per_z == 0:
            zl = z_chunks[n // per_z]
            z_ref[:, zl:zl + FF_CHUNK] = _dot(xe_ref[H0:, :], w_ref[:, zl:zl + FF_CHUNK]).astype(BF16)
        acc = cb_ref[:, sl] + cw_ref[n_sh:n_sh + 1, sl] * p[H0:, :]
        for k in range(n_sh):
            acc = acc + cw_ref[k:k + 1, sl] * p[H0 - n_sh + k:H0 - n_sh + k + tm, :]
        xc_ref[:, sl] = (acc * jax.nn.sigmoid(acc)).astype(BF16)
    for zl in z_chunks[-(-len(conv_chunks) // per_z):]:
        z_ref[:, zl:zl + FF_CHUNK] = _dot(xe_ref[H0:, :], w_ref[:, zl:zl + FF_CHUNK]).astype(BF16)
    dt_ref[...] = _dot(xe_ref[H0:, :], w_ref[:, inner + cc:])


def _ssd_proj(h, g, w, conv_w, conv_b, inner, cc):
    b, s, d = h.shape
    tm = min(TOKEN_TILE, s)
    per_halo = tm // CONV_HALO
    row = lambda n: pl.BlockSpec((None, tm, n), lambda i, j: (i, j, 0))
    return pl.pallas_call(
        _ssd_proj_kernel,
        grid=(b, s // tm),
        in_specs=[row(d),
                  pl.BlockSpec((None, CONV_HALO, d), lambda i, j: (i, jnp.maximum(j * per_halo - 1, 0), 0)),
                  _const_spec((1, d)), _const_spec(w.shape), _const_spec(conv_w.shape), _const_spec((1, cc))],
        out_specs=[row(inner), row(cc), row(LANES)],
        out_shape=[jax.ShapeDtypeStruct((b, s, inner), BF16), jax.ShapeDtypeStruct((b, s, cc), BF16),
                   jax.ShapeDtypeStruct((b, s, LANES), F32)],
        scratch_shapes=[pltpu.VMEM((CONV_HALO + tm, d), BF16)],
        compiler_params=_cparams("parallel", "arbitrary"),
        name="ssd_proj",
    )(h, h, g, w, conv_w, conv_b)


def _ssd_scan_kernel(xc_ref, dt_ref, dtb_ref, alog_ref, dskip_ref, y_ref, state_ref):
    L = xc_ref.shape[0]
    inner = y_ref.shape[1]
    G, N, P = SSD_GROUPS, SSD_STATE, SSD_HEADDIM
    hpg = inner // P // G

    @pl.when(pl.program_id(1) == 0)
    def _():
        state_ref[...] = jnp.zeros(state_ref.shape, F32)

    dt = _softplus(dt_ref[...] + dtb_ref[...])
    da = dt * (jnp.exp(alog_ref[...]) * -LOG2E)
    ti = lax.broadcasted_iota(jnp.int32, (L, L), 0)
    si = lax.broadcasted_iota(jnp.int32, (L, L), 1)
    causal = si <= ti
    tri = jnp.where(causal, 1.0, 0.0).astype(BF16)
    d1, d2, d3 = _split3(da)
    acum = _dot(tri, d1) + _dot(tri, d2) + _dot(tri, d3)
    acum_t = acum.T
    dt_t = dt.T
    dec_end_t = jnp.exp2(acum_t[:, L - 1:L] - acum_t) * dt_t
    e_last = jnp.exp2(acum[L - 1:L, :])
    lane = lax.broadcasted_iota(jnp.int32, (L, 2 * P), 1)
    first = lane < P
    lane1 = lax.broadcasted_iota(jnp.int32, (1, 2 * P), 1)
    first1 = lane1 < P

    for g in range(G):
        bg = xc_ref[:, inner + g * N:inner + (g + 1) * N]
        cg_b = xc_ref[:, inner + G * N + g * N:inner + G * N + (g + 1) * N]
        cg = cg_b.astype(F32)
        cbm = _dot_nt(cg_b, bg)
        bg_t = bg.astype(F32).T
        for pr in range(hpg // 2):
            h0 = g * hpg + 2 * pr
            xsl = slice(h0 * P, (h0 + 2) * P)
            psl = slice(2 * pr * P, (2 * pr + 2) * P)
            x_b = xc_ref[:, xsl]
            st_pair = state_ref[g, :, psl]
            rhs = jnp.concatenate([x_b, st_pair.astype(BF16)], axis=0)
            ys, us = [], []
            for h in (h0, h0 + 1):
                colb = jnp.broadcast_to(acum[:, h:h + 1], (L, L))
                decay = jnp.exp2(jnp.where(causal, colb - acum_t[h:h + 1, :], NEG_BIG))
                w = cbm * decay * dt_t[h:h + 1, :]
                cs = cg * jnp.exp2(colb)
                lhs = jnp.concatenate([w, cs], axis=1).astype(BF16)
                ys.append(_dot(lhs, rhs))
                us.append(_dot((bg_t * dec_end_t[h:h + 1, :]).astype(BF16), x_b))
            y_ref[:, xsl] = (jnp.where(first, ys[0], ys[1])
                             + dskip_ref[:, xsl] * x_b.astype(F32)).astype(y_ref.dtype)
            sd = jnp.where(first1, e_last[:, h0:h0 + 1], e_last[:, h0 + 1:h0 + 2])
            state_ref[g, :, psl] = st_pair * sd + jnp.where(first, us[0], us[1])


def _ssd_scan(xc, dt, dt_bias, a_log, d_skip, inner):
    b, s, cc = xc.shape
    L = min(SSD_L, s)
    G, N = SSD_GROUPS, SSD_STATE
    blk = lambda n: pl.BlockSpec((None, L, n), lambda i, j: (i, j, 0))
    return pl.pallas_call(
        _ssd_scan_kernel,
        grid=(b, s // L),
        in_specs=[blk(cc), blk(LANES), _const_spec((1, LANES)), _const_spec((1, LANES)),
                  _const_spec((1, inner))],
        out_specs=blk(inner),
        out_shape=jax.ShapeDtypeStruct((b, s, inner), BF16),
        scratch_shapes=[pltpu.VMEM((G, N, inner // G), F32)],
        compiler_params=_cparams("parallel", "arbitrary"),
        name="ssd_scan",
    )(xc, dt, dt_bias, a_log, d_skip)


OUT_ROW_BLOCKS = 2


def _ssd_out_kernel(y_ref, z_ref, ng_ref, w_ref, h_ref, g_ref, o_ref):
    tm = y_ref.shape[0]
    rb = tm // OUT_ROW_BLOCKS
    blocks = [slice(r0, r0 + rb) for r0 in range(0, tm, rb)]
    lhs = []
    for rs in blocks:
        z = z_ref[rs, :].astype(F32)
        yg = y_ref[rs, :].astype(F32) * (z * jax.nn.sigmoid(z))
        lhs.append(_rms(yg, ng_ref[...]).astype(BF16))
    ms = [_dot(x, w_ref[...]) for x in lhs]
    for rs, m in zip(blocks, ms):
        o_ref[rs, :] = h_ref[rs, :] + _rms(m, g_ref[...])


def _ssd_out(y, z, norm_g, w_out, h, g):
    t, d = h.shape
    inner = y.shape[1]
    tm = min(TOKEN_TILE, t)
    row = lambda n: pl.BlockSpec((tm, n), lambda i: (i, 0))
    return pl.pallas_call(
        _ssd_out_kernel,
        grid=(t // tm,),
        in_specs=[row(inner), row(inner), _const_spec((1, inner)), _const_spec(w_out.shape), row(d),
                  _const_spec((1, d))],
        out_specs=row(d),
        out_shape=jax.ShapeDtypeStruct((t, d), F32),
        compiler_params=_cparams("parallel"),
        name="ssd_out",
    )(y, z, norm_g, w_out, h, g)


def _pad_lanes(v):
    return jnp.zeros((1, LANES), F32).at[0, :v.shape[0]].set(v.astype(F32))


def _prep_ssd(w_in, conv_w, conv_b, dt_bias, a_log, d_skip, norm_g, w_out):
    d, n_in = w_in.shape
    n_heads = dt_bias.shape[0]
    inner = n_heads * SSD_HEADDIM
    cc = conv_w.shape[1]
    w = jnp.zeros((d, inner + cc + LANES), BF16).at[:, :n_in].set(w_in.astype(BF16))
    return (w, conv_w.astype(F32), conv_b.astype(F32)[None], _pad_lanes(dt_bias), _pad_lanes(a_log),
            jnp.repeat(d_skip.astype(F32), SSD_HEADDIM)[None], norm_g.astype(F32)[None], w_out.astype(BF16))


def _ssd_layer(h, g_pre, g_post, w, conv_w, conv_b, dt_bias, a_log, d_skip, norm_g, w_out):
    b, s, d = h.shape
    inner = norm_g.shape[-1]
    cc = conv_w.shape[1]
    z, xc, dt = _ssd_proj(h, g_pre, w, conv_w, conv_b, inner, cc)
    y = _ssd_scan(xc, dt, dt_bias, a_log, d_skip, inner)
    return _ssd_out(y.reshape(b * s, inner), z.reshape(b * s, inner), norm_g, w_out, h.reshape(b * s, d),
                    g_post).reshape(b, s, d)


SB_HEADS = 8
NSA_HEADS = 8
NSA_GROUPS = 2
NSA_REP = NSA_HEADS // NSA_GROUPS
CMP_LEN = 32
CMP_STRIDE = 16
SEL_BLOCK = 64
SEL_TOPN = 16
WINDOW = 512
FORCE_BONUS = 1e4
NSA_QB = 128
SB_W = SB_HEADS * HEAD_DIM
NSA_QW = NSA_HEADS * HEAD_DIM
NSA_KVW = NSA_GROUPS * HEAD_DIM
ATT_SCALE = HEAD_DIM ** -0.5 * LOG2E

_C_QSB, _C_KSB, _C_VSB = 0, SB_W, 2 * SB_W
_C_QN = 3 * SB_W
_C_K3 = _C_QN + NSA_QW
_C_V3 = _C_K3 + 3 * NSA_KVW
_C_GATE = _C_V3 + 3 * NSA_KVW
_C_END = _C_GATE + LANES


def _hyb_proj_kernel(h_ref, g_ref, pos_ref, inv_ref, sgn_ref, w_ref,
                     qsb_ref, ksb_ref, vsb_ref, qn_ref, kcmp_ref, vcmp_ref,
                     ksel_ref, vsel_ref, kwin_ref, vwin_ref, gate_ref):
    xn = _rms(h_ref[...], g_ref[...]).astype(BF16)
    tm = xn.shape[0]
    hd = HEAD_DIM

    def proj(lo, n):
        return _dot(xn, w_ref[:, lo:lo + n])

    for ref, lo, scale in ((qsb_ref, _C_QSB, ATT_SCALE), (ksb_ref, _C_KSB, 1.0), (vsb_ref, _C_VSB, 1.0)):
        p = proj(lo, SB_W) * scale
        for a in range(SB_HEADS):
            ref[a] = p[:, a * hd:(a + 1) * hd].astype(BF16)

    ang = pos_ref[...] * inv_ref[...]
    cos = jnp.cos(ang)
    sin = jnp.sin(ang) * sgn_ref[...]
    lane = lax.broadcasted_iota(jnp.int32, (tm, LANES), 1)
    low = lane < hd
    first_half = lane % hd < ROPE_HALF

    def rot(lo, n):
        p = proj(lo, n)
        tiles = []
        for c in range(n // LANES):
            x = p[:, c * LANES:(c + 1) * LANES]
            partner = jnp.where(first_half, pltpu.roll(x, LANES - ROPE_HALF, 1), pltpu.roll(x, ROPE_HALF, 1))
            tiles.append(x * cos + partner * sin)
        return jnp.concatenate(tiles, axis=1)

    qn = rot(_C_QN, NSA_QW) * ATT_SCALE
    for a in range(NSA_HEADS):
        grp = a // NSA_REP
        piece = qn[:, (a // 2) * LANES:(a // 2 + 1) * LANES]
        if (a % 2) != grp:
            piece = pltpu.roll(piece, hd, 1)
        keep = low if grp == 0 else jnp.logical_not(low)
        qn_ref[a] = jnp.where(keep, piece, 0.0).astype(BF16)

    k3 = rot(_C_K3, 3 * NSA_KVW)
    v3 = proj(_C_V3, 3 * NSA_KVW)
    for grp in range(NSA_GROUPS):
        kcmp_ref[grp] = k3[:, grp * hd:(grp + 1) * hd].astype(BF16)
        vcmp_ref[grp] = v3[:, grp * hd:(grp + 1) * hd].astype(BF16)
    ksel_ref[...] = k3[:, NSA_KVW:2 * NSA_KVW].astype(BF16)
    vsel_ref[...] = v3[:, NSA_KVW:2 * NSA_KVW].astype(BF16)
    kwin_ref[...] = k3[:, 2 * NSA_KVW:].astype(BF16)
    vwin_ref[...] = v3[:, 2 * NSA_KVW:].astype(BF16)
    gate_ref[...] = jax.nn.sigmoid(proj(_C_GATE, LANES))


def _hyb_proj(h, g, posf, inv, sgn, w):
    b, s, d = h.shape
    tm = min(TOKEN_TILE, s)
    heads = lambda n, w_: pl.BlockSpec((None, n, tm, w_), lambda i, j: (i, 0, j, 0))
    rows = lambda w_: pl.BlockSpec((None, tm, w_), lambda i, j: (i, j, 0))
    hshape = lambda n, w_: jax.ShapeDtypeStruct((b, n, s, w_), BF16)
    rshape = lambda w_, dt: jax.ShapeDtypeStruct((b, s, w_), dt)
    return pl.pallas_call(
        _hyb_proj_kernel,
        grid=(b, s // tm),
        in_specs=[rows(d), _const_spec((1, d)), rows(1), _const_spec((1, LANES)), _const_spec((1, LANES)),
                  _const_spec(w.shape)],
        out_specs=[heads(SB_HEADS, HEAD_DIM)] * 3 + [heads(NSA_HEADS, LANES)]
        + [heads(NSA_GROUPS, HEAD_DIM)] * 2 + [rows(LANES)] * 5,
        out_shape=[hshape(SB_HEADS, HEAD_DIM)] * 3 + [hshape(NSA_HEADS, LANES)]
        + [hshape(NSA_GROUPS, HEAD_DIM)] * 2 + [rshape(LANES, BF16)] * 4 + [rshape(LANES, F32)],
        compiler_params=_cparams("parallel", "parallel"),
        name="hyb_proj",
    )(h, g, posf, inv, sgn, w)


SB_TILE = 256
SB_SUB = 128
SB_DEAD_LOG2 = 160.0
SB_ROW_BANDS = 4


def _sb_blocks(heads, later2):
    jobs = [(h, off, k, v, mask) for h, (_, _, subs) in enumerate(heads) for off, k, v, mask in subs]
    ts = [_dot_nt(heads[h][0][off:], k) for h, off, k, _, _ in jobs]
    sps, hls = [], []
    for t, (_, _, _, _, mask) in zip(ts, jobs):
        neg_abs = lax.bitcast_convert_type(lax.bitcast_convert_type(t, jnp.uint32) | jnp.uint32(0x80000000),
                                           F32)
        sp = jnp.maximum(t, 0.0) + jnp.log2(1.0 + jnp.exp2(neg_abs))
        if mask is not None:
            sp = jnp.where(mask, sp, 0.0)
        hi = sp.astype(BF16)
        lo = (sp - hi.astype(F32)).astype(BF16)
        sps.append(sp)
        hls.append(jnp.concatenate([hi, lo], axis=1))
    css = [_dot(hl, later2) for hl in hls]
    carries = [carry for _, carry, _ in heads]
    ws = []
    for t, sp, cs, (h, off, _, _, mask) in zip(ts, sps, css, jobs):
        carry = carries[h]
        w = jnp.exp2(t - sp - cs - carry[off:])
        if mask is not None:
            w = jnp.where(mask, w, 0.0)
        ws.append(w.astype(BF16))
        grown = carry[off:] + cs[:, 0:1] + sp[:, 0:1]
        carries[h] = grown if off == 0 else jnp.concatenate([carry[:off], grown], axis=0)
    outs = [[] for _ in heads]
    for w, (h, off, _, v, _) in zip(ws, jobs):
        outs[h].append((off, _dot(w, v)))
    return list(zip(outs, carries))


def _sb_kernel(q_ref, kd_ref, vd_ref, k_hbm, v_hbm, o_ref, acc_ref, carry_ref, kbuf, vbuf, sem):
    bi = pl.program_id(0)
    i = pl.program_id(1)
    n_heads, tq, hd = q_ref.shape
    sub = min(SB_SUB, tq)
    n_sub = tq // sub
    n_prev = i

    def copies(n, slot):
        start = pl.multiple_of((n_prev - 1 - n) * tq, tq)
        return (pltpu.make_async_copy(k_hbm.at[bi, :, pl.ds(start, tq), :], kbuf.at[slot], sem.at[0, slot]),
                pltpu.make_async_copy(v_hbm.at[bi, :, pl.ds(start, tq), :], vbuf.at[slot], sem.at[1, slot]))

    def start_fetch(n, slot):
        for c in copies(n, slot):
            c.start()

    def wait_fetch(n, slot):
        for c in copies(n, slot):
            c.wait()

    @pl.when(n_prev > 0)
    def _():
        start_fetch(0, 0)

    kr = lax.broadcasted_iota(jnp.int32, (2 * sub, sub), 0) % sub
    kc = lax.broadcasted_iota(jnp.int32, (2 * sub, sub), 1)
    later2 = jnp.where(kr > kc, 1.0, 0.0).astype(BF16)

    acc_ref[...] = jnp.zeros(acc_ref.shape, F32)
    carry_ref[...] = jnp.zeros(carry_ref.shape, F32)

    def apply(results, rows):
        for a, (pvs, carry) in enumerate(results):
            for off, pv in pvs:
                acc_ref[a, off:rows, :] += pv
            carry_ref[a, :rows, :] = carry

    def diag_subs(a):
        subs = []
        for sb in reversed(range(n_sub)):
            r0 = sb * sub
            mask = (lax.broadcasted_iota(jnp.int32, (tq - r0, sub), 1)
                    < lax.broadcasted_iota(jnp.int32, (tq - r0, sub), 0))
            subs.append((r0, kd_ref[a, r0:r0 + sub, :], vd_ref[a, r0:r0 + sub, :], mask))
        return subs

    apply(_sb_blocks([(q_ref[a], carry_ref[a], diag_subs(a)) for a in range(n_heads)], later2), tq)

    band = tq // SB_ROW_BANDS

    def live_bands():
        least = carry_ref[0]
        for a in range(1, n_heads):
            least = jnp.minimum(least, carry_ref[a])
        band_no = lax.broadcasted_iota(jnp.int32, (tq, 1), 0) // band + 1
        return jnp.max(jnp.where(least < SB_DEAD_LOG2, band_no, 0))

    def body(state):
        n, n_live = state
        slot = n % 2
        wait_fetch(n, slot)

        @pl.when(n + 1 < n_prev)
        def _():
            start_fetch(n + 1, 1 - slot)

        for r in range(1, SB_ROW_BANDS + 1):
            @pl.when(n_live == r)
            def _(rows=r * band):
                def subs(a):
                    return [(0, kbuf[slot, a, sb * sub:(sb + 1) * sub, :], vbuf[slot, a, sb * sub:(sb + 1) * sub, :],
                             None) for sb in reversed(range(n_sub))]

                apply(_sb_blocks([(q_ref[a, :rows, :], carry_ref[a, :rows, :], subs(a))
                                  for a in range(n_heads)], later2), rows)
        return n + 1, live_bands()

    n_end, _ = lax.while_loop(lambda st: (st[0] < n_prev) & (st[1] > 0), body, (jnp.int32(0), live_bands()))

    @pl.when(n_end < n_prev)
    def _():
        wait_fetch(n_end, n_end % 2)

    for a in range(n_heads):
        o_ref[:, a * hd:(a + 1) * hd] = acc_ref[a].astype(o_ref.dtype)


def _sb_attn(q, k, v):
    b, n_heads, s, hd = q.shape
    t = min(SB_TILE, s)
    tile = pl.BlockSpec((None, n_heads, t, hd), lambda bi, i: (bi, 0, i, 0))
    hbm = pl.BlockSpec(memory_space=pl.ANY)
    return pl.pallas_call(
        _sb_kernel,
        grid=(b, s // t),
        in_specs=[tile, tile, tile, hbm, hbm],
        out_specs=pl.BlockSpec((None, t, n_heads * hd), lambda bi, i: (bi, i, 0)),
        out_shape=jax.ShapeDtypeStruct((b, s, n_heads * hd), BF16),
        scratch_shapes=[pltpu.VMEM((n_heads, t, hd), F32), pltpu.VMEM((n_heads, t, 1), F32),
                        pltpu.VMEM((2, n_heads, t, hd), BF16), pltpu.VMEM((2, n_heads, t, hd), BF16),
                        pltpu.SemaphoreType.DMA((2, 2))],
        compiler_params=_cparams("parallel", "arbitrary"),
        name="sb_attn",
    )(q, k, v, k, v)


def _compress_kernel(t_ref, wab_ref, pe_ref, w1_ref, w2_ref, o_ref):
    n_grp, nch, _ = t_ref.shape
    hid = w2_ref.shape[0]
    pe8 = jnp.broadcast_to(pe_ref[...], (8, pe_ref.shape[1])).astype(BF16)
    bias = _dot(pe8, w1_ref[...])[0:1, :]
    for grp in range(n_grp):
        ab = _dot(t_ref[grp], wab_ref[...])
        nxt = pltpu.roll(ab[:, hid:], nch - 1, 0)
        hcur = ab[:, :hid] + nxt + bias
        act = (hcur * jax.nn.sigmoid(hcur)).astype(BF16)
        o_ref[:, grp * HEAD_DIM:(grp + 1) * HEAD_DIM] = _dot(act, w2_ref[...]).astype(o_ref.dtype)


def _compress(t, wab, pe, w1, w2):
    b, n_grp, nch, kw = t.shape
    return pl.pallas_call(
        _compress_kernel,
        grid=(b,),
        in_specs=[pl.BlockSpec((None, n_grp, nch, kw), lambda i: (i, 0, 0, 0)), _const_spec(wab.shape),
                  _const_spec(pe.shape), _const_spec(w1.shape), _const_spec(w2.shape)],
        out_specs=pl.BlockSpec((None, nch, n_grp * HEAD_DIM), lambda i: (i, 0, 0)),
        out_shape=jax.ShapeDtypeStruct((b, nch, n_grp * HEAD_DIM), BF16),
        compiler_params=_cparams("parallel"),
        name="nsa_compress",
    )(t, wab, pe, w1, w2)


def _masked_softmax_rows(s, mask):
    sm = jnp.where(mask, s, NEG_BIG)
    m = jnp.max(sm, axis=-1, keepdims=True)
    e = jnp.where(mask, jnp.exp2(sm - m), 0.0)
    den = jnp.sum(e, axis=-1, keepdims=True)
    return e / jnp.maximum(den, 1e-30)


def _softmax_num(s, bias, groups):
    rows, n = s.shape
    sm = (s.reshape(groups, rows // groups, n) + bias[None]).reshape(rows, n)
    e = jnp.exp2(sm - jnp.max(sm, axis=-1, keepdims=True))
    return e.astype(BF16), jnp.sum(e, axis=-1, keepdims=True)


RANK_UNROLL = 8


def _select_kernel(qf_ref, kc_ref, o_ref):
    n_heads, nsub, _ = qf_ref.shape
    nch = kc_ref.shape[0]
    q = qf_ref[...].reshape(n_heads * nsub, LANES)
    s = _dot_nt(q, kc_ref[...])
    sub_id = lax.broadcasted_iota(jnp.int32, (n_heads * nsub, nch), 0) % nsub
    c_id = lax.broadcasted_iota(jnp.int32, (n_heads * nsub, nch), 1)
    p = _masked_softmax_rows(s, c_id * CMP_STRIDE + (CMP_LEN - 1) <= sub_id * SEL_BLOCK)
    rows = NSA_GROUPS * nsub
    psum = jnp.concatenate(
        [sum(p[(grp * NSA_REP + r) * nsub:(grp * NSA_REP + r + 1) * nsub] for r in range(NSA_REP))
         for grp in range(NSA_GROUPS)], axis=0)
    oc = lax.broadcasted_iota(jnp.int32, (nch, LANES), 0) * CMP_STRIDE
    oj = lax.broadcasted_iota(jnp.int32, (nch, LANES), 1) * SEL_BLOCK
    overlap = jnp.where((oc < oj + SEL_BLOCK) & (oc + (CMP_LEN - 1) >= oj), 1.0, 0.0).astype(BF16)
    p1, p2, p3 = _split3(psum)
    imp = _dot(p1, overlap) + _dot(p2, overlap) + _dot(p3, overlap)
    jl = lax.broadcasted_iota(jnp.int32, (rows, LANES), 1)
    cur = lax.broadcasted_iota(jnp.int32, (rows, LANES), 0) % nsub
    forced = (jl == 0) | (jl == cur) | (jl == cur - 1)
    score = jnp.where(jl <= cur, imp + jnp.where(forced, FORCE_BONUS, 0.0), -1.0)

    def body(d, rank):
        other = pltpu.roll(score, d, 1)
        ahead = (other > score) | ((other == score) & (jl >= d))
        return rank + jnp.where(ahead, 1.0, 0.0)

    rank = lax.fori_loop(1, LANES, body, jnp.zeros((rows, LANES), F32), unroll=RANK_UNROLL)
    jf = jl.astype(F32)
    out = jnp.zeros((rows, LANES), F32)
    for r in range(SEL_TOPN):
        col = jnp.sum(jnp.where(rank == float(r), jf, 0.0), axis=-1, keepdims=True)
        out = jnp.where(jl == r, col, out)
    o_ref[...] = out.astype(jnp.int32)


def _nsa_select(qf, kc):
    b, n_heads, nsub, _ = qf.shape
    nch = kc.shape[1]
    assert SEL_TOPN <= nsub <= LANES
    return pl.pallas_call(
        _select_kernel,
        grid=(b,),
        in_specs=[pl.BlockSpec((None, n_heads, nsub, LANES), lambda i: (i, 0, 0, 0)),
                  pl.BlockSpec((None, nch, LANES), lambda i: (i, 0, 0))],
        out_specs=pl.BlockSpec((None, NSA_GROUPS * nsub, LANES), lambda i: (i, 0, 0)),
        out_shape=jax.ShapeDtypeStruct((b, NSA_GROUPS * nsub, LANES), jnp.int32),
        compiler_params=_cparams("parallel"),
        name="nsa_select",
    )(qf, kc)


NSA_LOOKAHEAD = 2


def _nsa_kernel(sel_ref, qn_ref, kc_ref, vc_ref, ksel_ref, vsel_ref, kwin_ref, vwin_ref, gate_ref,
                o_ref, kg_ref, vg_ref):
    bi = pl.program_id(0)
    i = pl.program_id(1)
    n_heads, qb, _ = qn_ref.shape
    s_len = ksel_ref.shape[0]
    nsub_total = s_len // SEL_BLOCK
    nch = kc_ref.shape[0]
    hd = HEAD_DIM
    q = qn_ref[...].reshape(n_heads * qb, LANES)
    qpos = i * qb + lax.broadcasted_iota(jnp.int32, (qb, 1), 0)

    def bias_of(mask):
        return jnp.where(mask, 0.0, NEG_BIG)

    nsel = SEL_TOPN * SEL_BLOCK
    lane_blk = lax.broadcasted_iota(jnp.int32, (1, nsel), 1) // SEL_BLOCK
    lane_off = lax.broadcasted_iota(jnp.int32, (1, nsel), 1) % SEL_BLOCK
    n_sub = qb // SEL_BLOCK
    sel_jobs = []
    for grp in range(NSA_GROUPS):
        for n in range(n_sub):
            slot = grp * n_sub + n
            base = ((bi * NSA_GROUPS + grp) * nsub_total + i * n_sub + n) * SEL_TOPN
            tok = lane_off
            for t in range(SEL_TOPN):
                idx = sel_ref[base + t]
                off = pl.multiple_of(idx * SEL_BLOCK, SEL_BLOCK)
                kg_ref[slot, t * SEL_BLOCK:(t + 1) * SEL_BLOCK, :] = ksel_ref[pl.ds(off, SEL_BLOCK), :]
                vg_ref[slot, t * SEL_BLOCK:(t + 1) * SEL_BLOCK, :] = vsel_ref[pl.ds(off, SEL_BLOCK), :]
                tok = tok + jnp.where(lane_blk == t, idx * SEL_BLOCK, 0)
            qs = qn_ref[grp * NSA_REP:(grp + 1) * NSA_REP, n * SEL_BLOCK:(n + 1) * SEL_BLOCK, :]
            qp = i * qb + n * SEL_BLOCK + lax.broadcasted_iota(jnp.int32, (SEL_BLOCK, 1), 0)
            sel_jobs.append((slot, qs.reshape(NSA_REP * SEL_BLOCK, LANES), bias_of(tok <= qp)))

    c_end = lax.broadcasted_iota(jnp.int32, (1, nch), 1) * CMP_STRIDE + (CMP_LEN - 1)
    wlen = WINDOW + qb
    start = pl.multiple_of(jnp.maximum(i * qb - WINDOW, 0), qb)
    delta = qpos - (start + lax.broadcasted_iota(jnp.int32, (1, wlen), 1))

    jobs = [(lambda: _dot_nt(q, kc_ref[...]), bias_of(c_end <= qpos), n_heads, lambda: vc_ref[...]),
            (lambda: _dot_nt(q, kwin_ref[pl.ds(start, wlen), :]), bias_of((delta >= 0) & (delta < WINDOW)),
             n_heads, lambda: vwin_ref[pl.ds(start, wlen), :])]
    for slot, qs, bias in sel_jobs:
        jobs.append((lambda qs=qs, slot=slot: _dot_nt(qs, kg_ref[slot]), bias, NSA_REP,
                     lambda slot=slot: vg_ref[slot]))
    scores, nums, outs = {}, {}, []
    for step in range(len(jobs) + NSA_LOOKAHEAD + 1):
        if step < len(jobs):
            scores[step] = jobs[step][0]()
        j = step - NSA_LOOKAHEAD
        if 0 <= j < len(jobs):
            nums[j] = _softmax_num(scores.pop(j), jobs[j][1], jobs[j][2])
        j = step - NSA_LOOKAHEAD - 1
        if 0 <= j < len(jobs):
            e, den = nums.pop(j)
            outs.append(_dot(e, jobs[j][3]()) / den)
    any_c = jnp.where(qpos >= CMP_LEN - 1, 1.0, 0.0)
    o_c = (outs[0].reshape(n_heads, qb, LANES) * any_c[None]).reshape(n_heads * qb, LANES)
    o_w = outs[1]
    o_s = [[outs[2 + grp * n_sub + n] for n in range(n_sub)] for grp in range(NSA_GROUPS)]


    lane = lax.broadcasted_iota(jnp.int32, (qb, LANES), 1)
    low = lane < hd
    gates = gate_ref[...]
    mixed = []
    for a in range(n_heads):
        grp, r = a // NSA_REP, a % NSA_REP
        rs = slice(a * qb, (a + 1) * qb)
        sel_rows = jnp.concatenate([o_s[grp][n][r * SEL_BLOCK:(r + 1) * SEL_BLOCK] for n in range(n_sub)],
                                   axis=0)
        mixed.append(gates[:, 3 * a:3 * a + 1] * o_c[rs] + gates[:, 3 * a + 1:3 * a + 2] * sel_rows
                     + gates[:, 3 * a + 2:3 * a + 3] * o_w[rs])
    for pr in range(n_heads // 2):
        grp = (2 * pr) // NSA_REP
        left, right = mixed[2 * pr], mixed[2 * pr + 1]
        if grp == 0:
            right = pltpu.roll(right, hd, 1)
        else:
            left = pltpu.roll(left, hd, 1)
        o_ref[:, pr * LANES:(pr + 1) * LANES] = jnp.where(low, left, right).astype(o_ref.dtype)


def _nsa_attn(sel, qn, kc, vc, ksel, vsel, kwin, vwin, gates):
    b, n_heads, s, _ = qn.shape
    nch = kc.shape[1]
    qb = min(NSA_QB, s)
    assert s >= WINDOW + qb
    full = lambda n: pl.BlockSpec((None, n, LANES), lambda bi, i, sel: (bi, 0, 0))
    grid_spec = pltpu.PrefetchScalarGridSpec(
        num_scalar_prefetch=1,
        grid=(b, s // qb),
        in_specs=[pl.BlockSpec((None, n_heads, qb, LANES), lambda bi, i, sel: (bi, 0, i, 0)),
                  full(nch), full(nch), full(s), full(s), full(s), full(s),
                  pl.BlockSpec((None, qb, LANES), lambda bi, i, sel: (bi, i, 0))],
        out_specs=pl.BlockSpec((None, qb, n_heads * HEAD_DIM), lambda bi, i, sel: (bi, i, 0)),
        scratch_shapes=[pltpu.VMEM((NSA_GROUPS * (qb // SEL_BLOCK), SEL_TOPN * SEL_BLOCK, LANES), BF16)] * 2,
    )
    return pl.pallas_call(
        _nsa_kernel,
        grid_spec=grid_spec,
        out_shape=jax.ShapeDtypeStruct((b, s, n_heads * HEAD_DIM), BF16),
        compiler_params=_cparams("parallel", "arbitrary"),
        name="nsa_attn",
    )(sel, qn, kc, vc, ksel, vsel, kwin, vwin, gates)


def _prep_hybrid(w_in, w_out, pe_k, w1_k, w2_k, pe_v, w1_v, w2_v):
    d = w_in.shape[0]
    hd = HEAD_DIM
    col = np.arange(w_in.shape[1])
    o_qn = 3 * SB_W
    o_kv = o_qn + NSA_QW
    kv = lambda n: col[o_kv + n * NSA_KVW:o_kv + (n + 1) * NSA_KVW]
    qn_c = col[o_qn:o_qn + NSA_QW]
    k3_c = np.concatenate([kv(0), kv(2), kv(4)])
    v3_c = np.concatenate([kv(1), kv(3), kv(5)])
    gate_c = col[o_kv + 6 * NSA_KVW:]
    order = np.concatenate([col[:o_qn], qn_c, k3_c, v3_c, gate_c])
    w = jnp.zeros((d, _C_END), BF16).at[:, :order.shape[0]].set(w_in[:, order].astype(BF16))
    lanes = np.arange(LANES)
    inv = (ROPE_THETA ** (-(lanes % ROPE_HALF).astype(np.float32) / ROPE_HALF)).astype(np.float32)[None]
    sgn = np.where(lanes % hd < ROPE_HALF, -1.0, 1.0).astype(np.float32)[None]

    def cmp_w(pe, w1, w2):
        half = w1.shape[0] // 2
        wab = jnp.concatenate([w1[:half], w1[half:]], axis=1).astype(BF16)
        return wab, pe.reshape(1, -1).astype(F32), w1.astype(BF16), w2.astype(BF16)

    return (w, jnp.asarray(inv), jnp.asarray(sgn), w_out[:SB_W].astype(BF16), w_out[SB_W:].astype(BF16),
            cmp_w(pe_k, w1_k, w2_k), cmp_w(pe_v, w1_v, w2_v))


def _hybrid_layer(h, positions, g_pre, g_post, w, inv, sgn, wo_sb, wo_nsa, cmp_k, cmp_v):
    b, s, d = h.shape
    posf = positions.astype(F32)[..., None]
    (qsb, ksb, vsb, qn, kcmp, vcmp, ksel, vsel, kwin, vwin, gates) = _hyb_proj(h, g_pre, posf, inv, sgn, w)
    o_sb = _sb_attn(qsb, ksb, vsb)
    half = CMP_LEN // 2
    chunks = lambda t: t.reshape(b, NSA_GROUPS, s // half, half * HEAD_DIM)
    kc = _compress(chunks(kcmp), *cmp_k)
    vc = _compress(chunks(vcmp), *cmp_v)
    sel = _nsa_select(qn[:, :, ::SEL_BLOCK, :], kc)
    sel = sel[:, :, :SEL_TOPN].reshape(-1)
    o_nsa = _nsa_attn(sel, qn, kc, vc, ksel, vsel, kwin, vwin, gates)
    hf = h.reshape(b * s, d)
    return _outproj([o_sb.reshape(b * s, SB_W), o_nsa.reshape(b * s, NSA_QW)], [wo_sb, wo_nsa], hf,
                    g_post).reshape(b, s, d)


CROSS_HEADS = 4


def kernel(x, mem, positions, norm_g, ffn1_w_gu, ffn1_w_down, ffn2_w_gu, ffn2_w_down, cross_wq, cross_wkv,
           cross_wo, hyb_w_in, hyb_w_out, cmp_pe_k, cmp_w1_k, cmp_w2_k, cmp_pe_v, cmp_w1_v, cmp_w2_v,
           ssd_w_in, ssd_conv_w, ssd_conv_b, ssd_dt_bias, ssd_A_log, ssd_D, ssd_norm_g, ssd_w_out):
    b, s, d = x.shape
    depth = norm_g.shape[0]
    h = x
    for i in range(depth):
        g = norm_g[i].astype(F32)[:, None, :]
        j = i // 2
        h = _ffn(h.reshape(b * s, d), g[0], g[1], ffn1_w_gu[i].astype(BF16),
                 ffn1_w_down[i].astype(BF16)).reshape(b, s, d)
        if i % 2 == 0:
            h = _hybrid_layer(h, positions, g[2], g[3],
                              *_prep_hybrid(hyb_w_in[j], hyb_w_out[j], cmp_pe_k[j], cmp_w1_k[j], cmp_w2_k[j],
                                            cmp_pe_v[j], cmp_w1_v[j], cmp_w2_v[j]))
        else:
            h = _ssd_layer(h, g[2], g[3], *_prep_ssd(ssd_w_in[j], ssd_conv_w[j], ssd_conv_b[j], ssd_dt_bias[j],
                                                     ssd_A_log[j], ssd_D[j], ssd_norm_g[j], ssd_w_out[j]))
        k, v = _memkv(mem, g[6], cross_wkv[i].astype(BF16))
        h = _cross(h, g[4], g[5], cross_wq[i].astype(BF16), k, v, cross_wo[i].astype(BF16), CROSS_HEADS)
        h = _ffn(h.reshape(b * s, d), g[7], g[8], ffn2_w_gu[i].astype(BF16),
                 ffn2_w_down[i].astype(BF16)).reshape(b, s, d)
    return h
```

```python
import functools
import math

import jax
import jax.numpy as jnp
import numpy as np
from jax import lax
from jax.experimental import pallas as pl
from jax.experimental.pallas import tpu as pltpu

F32 = jnp.float32
BF16 = jnp.bfloat16

RMS_EPS = 1e-6
LOG2E = 1.4426950408889634
ROPE_THETA = 10000.0
HEAD_DIM = 64
ROPE_HALF = HEAD_DIM // 2

V7X_VMEM_BYTES = 64 * 1024 * 1024
VMEM_LIMIT = (V7X_VMEM_BYTES * 3) // 4
LANES = 128

TOKEN_TILE = 512
FF_CHUNK = 512


def _cparams(*sem):
    return pltpu.CompilerParams(dimension_semantics=sem, vmem_limit_bytes=VMEM_LIMIT)


def _rms(x, g):
    ms = jnp.mean(x * x, axis=-1, keepdims=True)
    return x * lax.rsqrt(ms + RMS_EPS) * g


def _dot(a, b):
    return jnp.dot(a, b, preferred_element_type=F32)


def _dot_nt(a, b):
    return lax.dot_general(a, b, (((1,), (1,)), ((), ())), preferred_element_type=F32)


def _const_spec(shape):
    nd = len(shape)
    return pl.BlockSpec(shape, lambda *_: (0,) * nd)


FFN_ROW_BLOCKS = 1


def _ffn_kernel(h_ref, gpre_ref, gpost_ref, wgu_ref, wd_ref, o_ref):
    d_ff = wd_ref.shape[0]
    tm = h_ref.shape[0]
    rb = tm // FFN_ROW_BLOCKS
    blocks = [slice(r0, r0 + rb) for r0 in range(0, tm, rb)]
    xns = [_rms(h_ref[rs, :], gpre_ref[...]).astype(BF16) for rs in blocks]
    for rs, xn in zip(blocks, xns):
        acc = jnp.zeros((rb, h_ref.shape[1]), F32)
        for c in range(d_ff // FF_CHUNK):
            lo = c * FF_CHUNK
            gate = _dot(xn, wgu_ref[:, lo:lo + FF_CHUNK])
            up = _dot(xn, wgu_ref[:, d_ff + lo:d_ff + lo + FF_CHUNK])
            act = (gate * jax.nn.sigmoid(gate) * up).astype(BF16)
            acc = acc + _dot(act, wd_ref[lo:lo + FF_CHUNK, :])
        o_ref[rs, :] = h_ref[rs, :] + 0.5 * _rms(acc, gpost_ref[...])


def _ffn(h, g_pre, g_post, w_gu, w_down):
    t, d = h.shape
    d_ff = w_down.shape[0]
    tm = min(TOKEN_TILE * FFN_ROW_BLOCKS, t)
    return pl.pallas_call(
        _ffn_kernel,
        grid=(t // tm,),
        in_specs=[
            pl.BlockSpec((tm, d), lambda i: (i, 0)),
            _const_spec((1, d)),
            _const_spec((1, d)),
            _const_spec((d, 2 * d_ff)),
            _const_spec((d_ff, d)),
        ],
        out_specs=pl.BlockSpec((tm, d), lambda i: (i, 0)),
        out_shape=jax.ShapeDtypeStruct((t, d), F32),
        compiler_params=_cparams("parallel"),
        name="ffn",
    )(h, g_pre, g_post, w_gu, w_down)


def _memkv_kernel(mem_ref, g_ref, wkv_ref, k_ref, v_ref):
    d = mem_ref.shape[-1]
    mn = _rms(mem_ref[...], g_ref[...]).astype(BF16)
    kv = _dot(mn, wkv_ref[...])
    k_ref[...] = kv[:, :d].astype(BF16)
    v_ref[...] = kv[:, d:].astype(BF16)


def _memkv(mem, g_mem, wkv):
    b, m, d = mem.shape
    return pl.pallas_call(
        _memkv_kernel,
        grid=(b,),
        in_specs=[
            pl.BlockSpec((None, m, d), lambda i: (i, 0, 0)),
            _const_spec((1, d)),
            _const_spec((d, 2 * d)),
        ],
        out_specs=[pl.BlockSpec((None, m, d), lambda i: (i, 0, 0))] * 2,
        out_shape=[jax.ShapeDtypeStruct((b, m, d), BF16)] * 2,
        compiler_params=_cparams("parallel"),
        name="memkv",
    )(mem, g_mem, wkv)


def _cross_kernel(n_heads, h_ref, gpre_ref, gpost_ref, wq_ref, k_ref, v_ref, wo_ref, o_ref):
    h = h_ref[...]
    d = h.shape[-1]
    hd = d // n_heads
    hn = _rms(h, gpre_ref[...]).astype(BF16)
    q = (_dot(hn, wq_ref[...]) * (hd ** -0.5 * LOG2E)).astype(BF16)
    sls = [slice(a * hd, (a + 1) * hd) for a in range(n_heads)]
    ss = [_dot_nt(q[:, sl], k_ref[:, sl]) for sl in sls]
    es = [jnp.exp2(s - jnp.max(s, axis=-1, keepdims=True)) for s in ss]
    os_ = [(_dot(e.astype(BF16), v_ref[:, sl]) / jnp.sum(e, axis=-1, keepdims=True)).astype(BF16)
           for e, sl in zip(es, sls)]
    c = _dot(jnp.concatenate(os_, axis=1), wo_ref[...])
    o_ref[...] = h + _rms(c, gpost_ref[...])


def _cross(h, g_pre, g_post, wq, k, v, wo, n_heads):
    b, s, d = h.shape
    m = k.shape[1]
    tm = min(TOKEN_TILE, s)
    return pl.pallas_call(
        functools.partial(_cross_kernel, n_heads),
        grid=(b, s // tm),
        in_specs=[
            pl.BlockSpec((None, tm, d), lambda i, j: (i, j, 0)),
            _const_spec((1, d)),
            _const_spec((1, d)),
            _const_spec((d, d)),
            pl.BlockSpec((None, m, d), lambda i, j: (i, 0, 0)),
            pl.BlockSpec((None, m, d), lambda i, j: (i, 0, 0)),
            _const_spec((d, d)),
        ],
        out_specs=pl.BlockSpec((None, tm, d), lambda i, j: (i, j, 0)),
        out_shape=jax.ShapeDtypeStruct((b, s, d), F32),
        compiler_params=_cparams("parallel", "parallel"),
        name="cross",
    )(h, g_pre, g_post, wq, k, v, wo)


def _outproj_kernel(n_in, *refs):
    a_refs = refs[:n_in]
    w_refs = refs[n_in:2 * n_in]
    h_ref, g_ref, o_ref = refs[2 * n_in:]
    m = _dot(a_refs[0][...], w_refs[0][...])
    for a_ref, w_ref in zip(a_refs[1:], w_refs[1:]):
        m = m + _dot(a_ref[...], w_ref[...])
    o_ref[...] = h_ref[...] + _rms(m, g_ref[...])


def _outproj(acts, weights, h, g):
    t, d = h.shape
    tm = min(TOKEN_TILE, t)
    n_in = len(acts)
    return pl.pallas_call(
        functools.partial(_outproj_kernel, n_in),
        grid=(t // tm,),
        in_specs=(
            [pl.BlockSpec((tm, a.shape[1]), lambda i: (i, 0)) for a in acts]
            + [_const_spec(w.shape) for w in weights]
            + [pl.BlockSpec((tm, d), lambda i: (i, 0)), _const_spec((1, d))]
        ),
        out_specs=pl.BlockSpec((tm, d), lambda i: (i, 0)),
        out_shape=jax.ShapeDtypeStruct((t, d), F32),
        compiler_params=_cparams("parallel"),
        name="outproj",
    )(*acts, *weights, h, g)


SSD_HEADDIM = 64
SSD_GROUPS = 4
SSD_STATE = 128
SSD_CONV = 4
SSD_L = 128
SSD_CHUNKS_PER_STEP = 4
CONV_HALO = 16
NEG_BIG = -1e30


def _split3(x):
    a = x.astype(BF16)
    r = x - a.astype(F32)
    b = r.astype(BF16)
    c = (r - b.astype(F32)).astype(BF16)
    return a, b, c


def _softplus(x):
    return jnp.maximum(x, 0.0) + jnp.log(1.0 + jnp.exp(-jnp.abs(x)))


CONV_CHUNK = 256


def _ssd_proj_kernel(h_ref, halo_ref, g_ref, w_ref, cw_ref, cb_ref, z_ref, xc_ref, dt_ref, xe_ref):
    tm = h_ref.shape[0]
    inner = z_ref.shape[-1]
    cc = xc_ref.shape[-1]
    H0 = CONV_HALO
    n_sh = SSD_CONV - 1
    xe_ref[0:H0, :] = jnp.where(pl.program_id(1) > 0, _rms(halo_ref[...], g_ref[...]), 0.0).astype(BF16)
    xe_ref[H0:, :] = _rms(h_ref[...], g_ref[...]).astype(BF16)
    z_chunks = list(range(0, inner, FF_CHUNK))
    conv_chunks = list(range(0, cc, CONV_CHUNK))
    per_z = -(-len(conv_chunks) // len(z_chunks))
    for n, lo in enumerate(conv_chunks):
        sl = slice(lo, lo + CONV_CHUNK)
        p = _dot(xe_ref[...], w_ref[:, inner + lo:inner + lo + CONV_CHUNK])
        if n % per_z == 0:
            zl = z_chunks[n // per_z]
            z_ref[:, zl:zl + FF_CHUNK] = _dot(xe_ref[H0:, :], w_ref[:, zl:zl + FF_CHUNK]).astype(BF16)
        acc = cb_ref[:, sl] + cw_ref[n_sh:n_sh + 1, sl] * p[H0:, :]
        for k in range(n_sh):
            acc = acc + cw_ref[k:k + 1, sl] * p[H0 - n_sh + k:H0 - n_sh + k + tm, :]
        xc_ref[:, sl] = (acc * jax.nn.sigmoid(acc)).astype(BF16)
    for zl in z_chunks[-(-len(conv_chunks) // per_z):]:
        z_ref[:, zl:zl + FF_CHUNK] = _dot(xe_ref[H0:, :], w_ref[:, zl:zl + FF_CHUNK]).astype(BF16)
    dt_ref[...] = _dot(xe_ref[H0:, :], w_ref[:, inner + cc:])


def _ssd_proj(h, g, w, conv_w, conv_b, inner, cc):
    b, s, d = h.shape
    tm = min(TOKEN_TILE, s)
    per_halo = tm // CONV_HALO
    row = lambda n: pl.BlockSpec((None, tm, n), lambda i, j: (i, j, 0))
    return pl.pallas_call(
        _ssd_proj_kernel,
        grid=(b, s // tm),
        in_specs=[row(d),
                  pl.BlockSpec((None, CONV_HALO, d), lambda i, j: (i, jnp.maximum(j * per_halo - 1, 0), 0)),
                  _const_spec((1, d)), _const_spec(w.shape), _const_spec(conv_w.shape), _const_spec((1, cc))],
        out_specs=[row(inner), row(cc), row(LANES)],
        out_shape=[jax.ShapeDtypeStruct((b, s, inner), BF16), jax.ShapeDtypeStruct((b, s, cc), BF16),
                   jax.ShapeDtypeStruct((b, s, LANES), F32)],
        scratch_shapes=[pltpu.VMEM((CONV_HALO + tm, d), BF16)],
        compiler_params=_cparams("parallel", "arbitrary"),
        name="ssd_proj",
    )(h, h, g, w, conv_w, conv_b)


def _ssd_scan_kernel(xc_ref, dt_ref, dtb_ref, alog_ref, dskip_ref, y_ref, state_ref):
    L = min(SSD_L, xc_ref.shape[0])
    inner = y_ref.shape[1]

    @pl.when(pl.program_id(1) == 0)
    def _():
        state_ref[...] = jnp.zeros(state_ref.shape, F32)

    for c0 in range(0, xc_ref.shape[0], L):
        rows = slice(c0, c0 + L)
        _ssd_chunk(xc_ref.at[rows, :], dt_ref[rows, :], dtb_ref, alog_ref, dskip_ref, y_ref.at[rows, :],
                   state_ref, L, inner)


def _ssd_chunk(xc_ref, dt_raw, dtb_ref, alog_ref, dskip_ref, y_ref, state_ref, L, inner):
    G, N, P = SSD_GROUPS, SSD_STATE, SSD_HEADDIM
    hpg = inner // P // G

    dt = _softplus(dt_raw + dtb_ref[...])
    da = dt * (jnp.exp(alog_ref[...]) * -LOG2E)
    ti = lax.broadcasted_iota(jnp.int32, (L, L), 0)
    si = lax.broadcasted_iota(jnp.int32, (L, L), 1)
    causal = si <= ti
    tri = jnp.where(causal, 1.0, 0.0).astype(BF16)
    d1, d2, d3 = _split3(da)
    acum = _dot(tri, d1) + _dot(tri, d2) + _dot(tri, d3)
    acum_t = acum.T
    dt_t = dt.T
    dec_end_t = jnp.exp2(acum_t[:, L - 1:L] - acum_t) * dt_t
    e_last = jnp.exp2(acum[L - 1:L, :])
    lane = lax.broadcasted_iota(jnp.int32, (L, 2 * P), 1)
    first = lane < P
    lane1 = lax.broadcasted_iota(jnp.int32, (1, 2 * P), 1)
    first1 = lane1 < P

    for g in range(G):
        bg = xc_ref[:, inner + g * N:inner + (g + 1) * N]
        cg_b = xc_ref[:, inner + G * N + g * N:inner + G * N + (g + 1) * N]
        cg = cg_b.astype(F32)
        cbm = _dot_nt(cg_b, bg)
        bg_t = bg.astype(F32).T
        for pr in range(hpg // 2):
            h0 = g * hpg + 2 * pr
            xsl = slice(h0 * P, (h0 + 2) * P)
            psl = slice(2 * pr * P, (2 * pr + 2) * P)
            x_b = xc_ref[:, xsl]
            st_pair = state_ref[g, :, psl]
            rhs = jnp.concatenate([x_b, st_pair.astype(BF16)], axis=0)
            ys, us = [], []
            for h in (h0, h0 + 1):
                colb = jnp.broadcast_to(acum[:, h:h + 1], (L, L))
                decay = jnp.exp2(jnp.where(causal, colb - acum_t[h:h + 1, :], NEG_BIG))
                w = cbm * decay * dt_t[h:h + 1, :]
                cs = cg * jnp.exp2(colb)
                lhs = jnp.concatenate([w, cs], axis=1).astype(BF16)
                ys.append(_dot(lhs, rhs))
                us.append(_dot((bg_t * dec_end_t[h:h + 1, :]).astype(BF16), x_b))
            y_ref[:, xsl] = (jnp.where(first, ys[0], ys[1])
                             + dskip_ref[:, xsl] * x_b.astype(F32)).astype(y_ref.dtype)
            sd = jnp.where(first1, e_last[:, h0:h0 + 1], e_last[:, h0 + 1:h0 + 2])
            state_ref[g, :, psl] = st_pair * sd + jnp.where(first, us[0], us[1])


def _ssd_scan(xc, dt, dt_bias, a_log, d_skip, inner):
    b, s, cc = xc.shape
    L = min(SSD_L * SSD_CHUNKS_PER_STEP, s)
    G, N = SSD_GROUPS, SSD_STATE
    blk = lambda n: pl.BlockSpec((None, L, n), lambda i, j: (i, j, 0))
    return pl.pallas_call(
        _ssd_scan_kernel,
        grid=(b, s // L),
        in_specs=[blk(cc), blk(LANES), _const_spec((1, LANES)), _const_spec((1, LANES)),
                  _const_spec((1, inner))],
        out_specs=blk(inner),
        out_shape=jax.ShapeDtypeStruct((b, s, inner), BF16),
        scratch_shapes=[pltpu.VMEM((G, N, inner // G), F32)],
        compiler_params=_cparams("parallel", "arbitrary"),
        name="ssd_scan",
    )(xc, dt, dt_bias, a_log, d_skip)


OUT_ROW_BLOCKS = 2


def _ssd_out_kernel(y_ref, z_ref, ng_ref, w_ref, h_ref, g_ref, o_ref):
    tm = y_ref.shape[0]
    rb = tm // OUT_ROW_BLOCKS
    blocks = [slice(r0, r0 + rb) for r0 in range(0, tm, rb)]
    lhs = []
    for rs in blocks:
        z = z_ref[rs, :].astype(F32)
        yg = y_ref[rs, :].astype(F32) * (z * jax.nn.sigmoid(z))
        lhs.append(_rms(yg, ng_ref[...]).astype(BF16))
    ms = [_dot(x, w_ref[...]) for x in lhs]
    for rs, m in zip(blocks, ms):
        o_ref[rs, :] = h_ref[rs, :] + _rms(m, g_ref[...])


def _ssd_out(y, z, norm_g, w_out, h, g):
    t, d = h.shape
    inner = y.shape[1]
    tm = min(TOKEN_TILE, t)
    row = lambda n: pl.BlockSpec((tm, n), lambda i: (i, 0))
    return pl.pallas_call(
        _ssd_out_kernel,
        grid=(t // tm,),
        in_specs=[row(inner), row(inner), _const_spec((1, inner)), _const_spec(w_out.shape), row(d),
                  _const_spec((1, d))],
        out_specs=row(d),
        out_shape=jax.ShapeDtypeStruct((t, d), F32),
        compiler_params=_cparams("parallel"),
        name="ssd_out",
    )(y, z, norm_g, w_out, h, g)


def _pad_lanes(v):
    return jnp.zeros((1, LANES), F32).at[0, :v.shape[0]].set(v.astype(F32))


def _prep_ssd(w_in, conv_w, conv_b, dt_bias, a_log, d_skip, norm_g, w_out):
    d, n_in = w_in.shape
    n_heads = dt_bias.shape[0]
    inner = n_heads * SSD_HEADDIM
    cc = conv_w.shape[1]
    w = jnp.zeros((d, inner + cc + LANES), BF16).at[:, :n_in].set(w_in.astype(BF16))
    return (w, conv_w.astype(F32), conv_b.astype(F32)[None], _pad_lanes(dt_bias), _pad_lanes(a_log),
            jnp.repeat(d_skip.astype(F32), SSD_HEADDIM)[None], norm_g.astype(F32)[None], w_out.astype(BF16))


def _ssd_layer(h, g_pre, g_post, w, conv_w, conv_b, dt_bias, a_log, d_skip, norm_g, w_out):
    b, s, d = h.shape
    inner = norm_g.shape[-1]
    cc = conv_w.shape[1]
    z, xc, dt = _ssd_proj(h, g_pre, w, conv_w, conv_b, inner, cc)
    y = _ssd_scan(xc, dt, dt_bias, a_log, d_skip, inner)
    return _ssd_out(y.reshape(b * s, inner), z.reshape(b * s, inner), norm_g, w_out, h.reshape(b * s, d),
                    g_post).reshape(b, s, d)


SB_HEADS = 8
NSA_HEADS = 8
NSA_GROUPS = 2
NSA_REP = NSA_HEADS // NSA_GROUPS
CMP_LEN = 32
CMP_STRIDE = 16
SEL_BLOCK = 64
SEL_TOPN = 16
WINDOW = 512
FORCE_BONUS = 1e4
NSA_QB = 128
SB_W = SB_HEADS * HEAD_DIM
NSA_QW = NSA_HEADS * HEAD_DIM
NSA_KVW = NSA_GROUPS * HEAD_DIM
ATT_SCALE = HEAD_DIM ** -0.5 * LOG2E

_C_QSB, _C_KSB, _C_VSB = 0, SB_W, 2 * SB_W
_C_QN = 3 * SB_W
_C_K3 = _C_QN + NSA_QW
_C_V3 = _C_K3 + 3 * NSA_KVW
_C_GATE = _C_V3 + 3 * NSA_KVW
_C_END = _C_GATE + LANES


def _hyb_proj_kernel(h_ref, g_ref, pos_ref, inv_ref, sgn_ref, w_ref,
                     qsb_ref, ksb_ref, vsb_ref, qn_ref, kcmp_ref, vcmp_ref,
                     ksel_ref, vsel_ref, kwin_ref, vwin_ref, gate_ref):
    xn = _rms(h_ref[...], g_ref[...]).astype(BF16)
    tm = xn.shape[0]
    hd = HEAD_DIM

    def proj(lo, n):
        return _dot(xn, w_ref[:, lo:lo + n])

    for ref, lo, scale in ((qsb_ref, _C_QSB, ATT_SCALE), (ksb_ref, _C_KSB, 1.0), (vsb_ref, _C_VSB, 1.0)):
        p = proj(lo, SB_W) * scale
        for a in range(SB_HEADS):
            ref[a] = p[:, a * hd:(a + 1) * hd].astype(BF16)

    ang = pos_ref[...] * inv_ref[...]
    cos = jnp.cos(ang)
    sin = jnp.sin(ang) * sgn_ref[...]
    lane = lax.broadcasted_iota(jnp.int32, (tm, LANES), 1)
    low = lane < hd
    first_half = lane % hd < ROPE_HALF

    def rot(lo, n):
        p = proj(lo, n)
        tiles = []
        for c in range(n // LANES):
            x = p[:, c * LANES:(c + 1) * LANES]
            partner = jnp.where(first_half, pltpu.roll(x, LANES - ROPE_HALF, 1), pltpu.roll(x, ROPE_HALF, 1))
            tiles.append(x * cos + partner * sin)
        return jnp.concatenate(tiles, axis=1)

    qn = rot(_C_QN, NSA_QW) * ATT_SCALE
    for a in range(NSA_HEADS):
        grp = a // NSA_REP
        piece = qn[:, (a // 2) * LANES:(a // 2 + 1) * LANES]
        if (a % 2) != grp:
            piece = pltpu.roll(piece, hd, 1)
        keep = low if grp == 0 else jnp.logical_not(low)
        qn_ref[a] = jnp.where(keep, piece, 0.0).astype(BF16)

    k3 = rot(_C_K3, 3 * NSA_KVW)
    v3 = proj(_C_V3, 3 * NSA_KVW)
    for grp in range(NSA_GROUPS):
        kcmp_ref[grp] = k3[:, grp * hd:(grp + 1) * hd].astype(BF16)
        vcmp_ref[grp] = v3[:, grp * hd:(grp + 1) * hd].astype(BF16)
    ksel_ref[...] = k3[:, NSA_KVW:2 * NSA_KVW].astype(BF16)
    vsel_ref[...] = v3[:, NSA_KVW:2 * NSA_KVW].astype(BF16)
    kwin_ref[...] = k3[:, 2 * NSA_KVW:].astype(BF16)
    vwin_ref[...] = v3[:, 2 * NSA_KVW:].astype(BF16)
    gate_ref[...] = jax.nn.sigmoid(proj(_C_GATE, LANES))


def _hyb_proj(h, g, posf, inv, sgn, w):
    b, s, d = h.shape
    tm = min(TOKEN_TILE, s)
    heads = lambda n, w_: pl.BlockSpec((None, n, tm, w_), lambda i, j: (i, 0, j, 0))
    rows = lambda w_: pl.BlockSpec((None, tm, w_), lambda i, j: (i, j, 0))
    hshape = lambda n, w_: jax.ShapeDtypeStruct((b, n, s, w_), BF16)
    rshape = lambda w_, dt: jax.ShapeDtypeStruct((b, s, w_), dt)
    return pl.pallas_call(
        _hyb_proj_kernel,
        grid=(b, s // tm),
        in_specs=[rows(d), _const_spec((1, d)), rows(1), _const_spec((1, LANES)), _const_spec((1, LANES)),
                  _const_spec(w.shape)],
        out_specs=[heads(SB_HEADS, HEAD_DIM)] * 3 + [heads(NSA_HEADS, LANES)]
        + [heads(NSA_GROUPS, HEAD_DIM)] * 2 + [rows(LANES)] * 5,
        out_shape=[hshape(SB_HEADS, HEAD_DIM)] * 3 + [hshape(NSA_HEADS, LANES)]
        + [hshape(NSA_GROUPS, HEAD_DIM)] * 2 + [rshape(LANES, BF16)] * 4 + [rshape(LANES, F32)],
        compiler_params=_cparams("parallel", "parallel"),
        name="hyb_proj",
    )(h, g, posf, inv, sgn, w)


SB_TILE = 256
SB_SUB = 128
SB_DEAD_LOG2 = 160.0
SB_ROW_BANDS = 4


def _sb_blocks(heads, later2):
    jobs = [(h, off, k, v, mask) for h, (_, _, subs) in enumerate(heads) for off, k, v, mask in subs]
    ts = [_dot_nt(heads[h][0][off:], k) for h, off, k, _, _ in jobs]
    sps, hls = [], []
    for t, (_, _, _, _, mask) in zip(ts, jobs):
        neg_abs = lax.bitcast_convert_type(lax.bitcast_convert_type(t, jnp.uint32) | jnp.uint32(0x80000000),
                                           F32)
        sp = jnp.maximum(t, 0.0) + jnp.log2(1.0 + jnp.exp2(neg_abs))
        if mask is not None:
            sp = jnp.where(mask, sp, 0.0)
        hi = sp.astype(BF16)
        lo = (sp - hi.astype(F32)).astype(BF16)
        sps.append(sp)
        hls.append(jnp.concatenate([hi, lo], axis=1))
    css = [_dot(hl, later2) for hl in hls]
    carries = [carry for _, carry, _ in heads]
    ws = []
    for t, sp, cs, (h, off, _, _, mask) in zip(ts, sps, css, jobs):
        carry = carries[h]
        w = jnp.exp2(t - sp - cs - carry[off:])
        if mask is not None:
            w = jnp.where(mask, w, 0.0)
        ws.append(w.astype(BF16))
        grown = carry[off:] + cs[:, 0:1] + sp[:, 0:1]
        carries[h] = grown if off == 0 else jnp.concatenate([carry[:off], grown], axis=0)
    outs = [[] for _ in heads]
    for w, (h, off, _, v, _) in zip(ws, jobs):
        outs[h].append((off, _dot(w, v)))
    return list(zip(outs, carries))


def _sb_kernel(q_ref, kd_ref, vd_ref, k_hbm, v_hbm, o_ref, acc_ref, carry_ref, kbuf, vbuf, sem):
    bi = pl.program_id(0)
    i = pl.program_id(1)
    n_heads, tq, hd = q_ref.shape
    sub = min(SB_SUB, tq)
    n_sub = tq // sub
    n_prev = i

    def copies(n, slot):
        start = pl.multiple_of((n_prev - 1 - n) * tq, tq)
        return (pltpu.make_async_copy(k_hbm.at[bi, :, pl.ds(start, tq), :], kbuf.at[slot], sem.at[0, slot]),
                pltpu.make_async_copy(v_hbm.at[bi, :, pl.ds(start, tq), :], vbuf.at[slot], sem.at[1, slot]))

    def start_fetch(n, slot):
        for c in copies(n, slot):
            c.start()

    def wait_fetch(n, slot):
        for c in copies(n, slot):
            c.wait()

    @pl.when(n_prev > 0)
    def _():
        start_fetch(0, 0)

    kr = lax.broadcasted_iota(jnp.int32, (2 * sub, sub), 0) % sub
    kc = lax.broadcasted_iota(jnp.int32, (2 * sub, sub), 1)
    later2 = jnp.where(kr > kc, 1.0, 0.0).astype(BF16)

    acc_ref[...] = jnp.zeros(acc_ref.shape, F32)
    carry_ref[...] = jnp.zeros(carry_ref.shape, F32)

    def apply(results, rows):
        for a, (pvs, carry) in enumerate(results):
            for off, pv in pvs:
                acc_ref[a, off:rows, :] += pv
            carry_ref[a, :rows, :] = carry

    def diag_subs(a):
        subs = []
        for sb in reversed(range(n_sub)):
            r0 = sb * sub
            mask = (lax.broadcasted_iota(jnp.int32, (tq - r0, sub), 1)
                    < lax.broadcasted_iota(jnp.int32, (tq - r0, sub), 0))
            subs.append((r0, kd_ref[a, r0:r0 + sub, :], vd_ref[a, r0:r0 + sub, :], mask))
        return subs

    apply(_sb_blocks([(q_ref[a], carry_ref[a], diag_subs(a)) for a in range(n_heads)], later2), tq)

    band = tq // SB_ROW_BANDS

    def live_bands():
        least = carry_ref[0]
        for a in range(1, n_heads):
            least = jnp.minimum(least, carry_ref[a])
        band_no = lax.broadcasted_iota(jnp.int32, (tq, 1), 0) // band + 1
        return jnp.max(jnp.where(least < SB_DEAD_LOG2, band_no, 0))

    def body(state):
        n, n_live = state
        slot = n % 2
        wait_fetch(n, slot)

        @pl.when(n + 1 < n_prev)
        def _():
            start_fetch(n + 1, 1 - slot)

        for r in range(1, SB_ROW_BANDS + 1):
            @pl.when(n_live == r)
            def _(rows=r * band):
                def subs(a):
                    return [(0, kbuf[slot, a, sb * sub:(sb + 1) * sub, :], vbuf[slot, a, sb * sub:(sb + 1) * sub, :],
                             None) for sb in reversed(range(n_sub))]

                apply(_sb_blocks([(q_ref[a, :rows, :], carry_ref[a, :rows, :], subs(a))
                                  for a in range(n_heads)], later2), rows)
        return n + 1, live_bands()

    n_end, _ = lax.while_loop(lambda st: (st[0] < n_prev) & (st[1] > 0), body, (jnp.int32(0), live_bands()))

    @pl.when(n_end < n_prev)
    def _():
        wait_fetch(n_end, n_end % 2)

    for a in range(n_heads):
        o_ref[:, a * hd:(a + 1) * hd] = acc_ref[a].astype(o_ref.dtype)


def _sb_attn(q, k, v):
    b, n_heads, s, hd = q.shape
    t = min(SB_TILE, s)
    tile = pl.BlockSpec((None, n_heads, t, hd), lambda bi, i: (bi, 0, i, 0))
    hbm = pl.BlockSpec(memory_space=pl.ANY)
    return pl.pallas_call(
        _sb_kernel,
        grid=(b, s // t),
        in_specs=[tile, tile, tile, hbm, hbm],
        out_specs=pl.BlockSpec((None, t, n_heads * hd), lambda bi, i: (bi, i, 0)),
        out_shape=jax.ShapeDtypeStruct((b, s, n_heads * hd), BF16),
        scratch_shapes=[pltpu.VMEM((n_heads, t, hd), F32), pltpu.VMEM((n_heads, t, 1), F32),
                        pltpu.VMEM((2, n_heads, t, hd), BF16), pltpu.VMEM((2, n_heads, t, hd), BF16),
                        pltpu.SemaphoreType.DMA((2, 2))],
        compiler_params=_cparams("parallel", "arbitrary"),
        name="sb_attn",
    )(q, k, v, k, v)


def _compress_kernel(t_ref, wab_ref, pe_ref, w1_ref, w2_ref, o_ref):
    n_grp, nch, _ = t_ref.shape
    hid = w2_ref.shape[0]
    pe8 = jnp.broadcast_to(pe_ref[...], (8, pe_ref.shape[1])).astype(BF16)
    bias = _dot(pe8, w1_ref[...])[0:1, :]
    for grp in range(n_grp):
        ab = _dot(t_ref[grp], wab_ref[...])
        nxt = pltpu.roll(ab[:, hid:], nch - 1, 0)
        hcur = ab[:, :hid] + nxt + bias
        act = (hcur * jax.nn.sigmoid(hcur)).astype(BF16)
        o_ref[:, grp * HEAD_DIM:(grp + 1) * HEAD_DIM] = _dot(act, w2_ref[...]).astype(o_ref.dtype)


def _compress(t, wab, pe, w1, w2):
    b, n_grp, nch, kw = t.shape
    return pl.pallas_call(
        _compress_kernel,
        grid=(b,),
        in_specs=[pl.BlockSpec((None, n_grp, nch, kw), lambda i: (i, 0, 0, 0)), _const_spec(wab.shape),
                  _const_spec(pe.shape), _const_spec(w1.shape), _const_spec(w2.shape)],
        out_specs=pl.BlockSpec((None, nch, n_grp * HEAD_DIM), lambda i: (i, 0, 0)),
        out_shape=jax.ShapeDtypeStruct((b, nch, n_grp * HEAD_DIM), BF16),
        compiler_params=_cparams("parallel"),
        name="nsa_compress",
    )(t, wab, pe, w1, w2)


def _masked_softmax_rows(s, mask):
    sm = jnp.where(mask, s, NEG_BIG)
    m = jnp.max(sm, axis=-1, keepdims=True)
    e = jnp.where(mask, jnp.exp2(sm - m), 0.0)
    den = jnp.sum(e, axis=-1, keepdims=True)
    return e / jnp.maximum(den, 1e-30)


def _softmax_num(s, bias, groups):
    rows, n = s.shape
    sm = (s.reshape(groups, rows // groups, n) + bias[None]).reshape(rows, n)
    e = jnp.exp2(sm - jnp.max(sm, axis=-1, keepdims=True))
    return e.astype(BF16), jnp.sum(e, axis=-1, keepdims=True)


RANK_UNROLL = 8


def _select_kernel(qf_ref, kc_ref, o_ref):
    n_heads, nsub, _ = qf_ref.shape
    nch = kc_ref.shape[0]
    q = qf_ref[...].reshape(n_heads * nsub, LANES)
    s = _dot_nt(q, kc_ref[...])
    sub_id = lax.broadcasted_iota(jnp.int32, (n_heads * nsub, nch), 0) % nsub
    c_id = lax.broadcasted_iota(jnp.int32, (n_heads * nsub, nch), 1)
    p = _masked_softmax_rows(s, c_id * CMP_STRIDE + (CMP_LEN - 1) <= sub_id * SEL_BLOCK)
    rows = NSA_GROUPS * nsub
    psum = jnp.concatenate(
        [sum(p[(grp * NSA_REP + r) * nsub:(grp * NSA_REP + r + 1) * nsub] for r in range(NSA_REP))
         for grp in range(NSA_GROUPS)], axis=0)
    oc = lax.broadcasted_iota(jnp.int32, (nch, LANES), 0) * CMP_STRIDE
    oj = lax.broadcasted_iota(jnp.int32, (nch, LANES), 1) * SEL_BLOCK
    overlap = jnp.where((oc < oj + SEL_BLOCK) & (oc + (CMP_LEN - 1) >= oj), 1.0, 0.0).astype(BF16)
    p1, p2, p3 = _split3(psum)
    imp = _dot(p1, overlap) + _dot(p2, overlap) + _dot(p3, overlap)
    jl = lax.broadcasted_iota(jnp.int32, (rows, LANES), 1)
    cur = lax.broadcasted_iota(jnp.int32, (rows, LANES), 0) % nsub
    forced = (jl == 0) | (jl == cur) | (jl == cur - 1)
    score = jnp.where(jl <= cur, imp + jnp.where(forced, FORCE_BONUS, 0.0), -1.0)

    def body(d, rank):
        other = pltpu.roll(score, d, 1)
        ahead = (other > score) | ((other == score) & (jl >= d))
        return rank + jnp.where(ahead, 1.0, 0.0)

    rank = lax.fori_loop(1, LANES, body, jnp.zeros((rows, LANES), F32), unroll=RANK_UNROLL)
    jf = jl.astype(F32)
    out = jnp.zeros((rows, LANES), F32)
    for r in range(SEL_TOPN):
        col = jnp.sum(jnp.where(rank == float(r), jf, 0.0), axis=-1, keepdims=True)
        out = jnp.where(jl == r, col, out)
    o_ref[...] = out.astype(jnp.int32)


def _nsa_select(qf, kc):
    b, n_heads, nsub, _ = qf.shape
    nch = kc.shape[1]
    assert SEL_TOPN <= nsub <= LANES
    return pl.pallas_call(
        _select_kernel,
        grid=(b,),
        in_specs=[pl.BlockSpec((None, n_heads, nsub, LANES), lambda i: (i, 0, 0, 0)),
                  pl.BlockSpec((None, nch, LANES), lambda i: (i, 0, 0))],
        out_specs=pl.BlockSpec((None, NSA_GROUPS * nsub, LANES), lambda i: (i, 0, 0)),
        out_shape=jax.ShapeDtypeStruct((b, NSA_GROUPS * nsub, LANES), jnp.int32),
        compiler_params=_cparams("parallel"),
        name="nsa_select",
    )(qf, kc)


NSA_LOOKAHEAD = 2
NSA_BLOCKS_PER_STEP = 1


def _nsa_kernel(sel_ref, qn_ref, kc_ref, vc_ref, ksel_ref, vsel_ref, kwin_ref, vwin_ref, gate_ref,
                o_ref, kg_ref, vg_ref):
    n_blk = kg_ref.shape[0]
    qb = qn_ref.shape[1] // n_blk
    for k in range(n_blk):
        rows = slice(k * qb, (k + 1) * qb)
        _nsa_block(pl.program_id(1) * n_blk + k, sel_ref, qn_ref.at[:, rows, :], kc_ref, vc_ref, ksel_ref,
                   vsel_ref, kwin_ref, vwin_ref, gate_ref.at[rows, :], o_ref.at[rows, :], kg_ref.at[k],
                   vg_ref.at[k])


def _nsa_block(i, sel_ref, qn_ref, kc_ref, vc_ref, ksel_ref, vsel_ref, kwin_ref, vwin_ref, gate_ref,
               o_ref, kg_ref, vg_ref):
    bi = pl.program_id(0)
    n_heads, qb, _ = qn_ref.shape
    s_len = ksel_ref.shape[0]
    nsub_total = s_len // SEL_BLOCK
    nch = kc_ref.shape[0]
    hd = HEAD_DIM
    q = qn_ref[...].reshape(n_heads * qb, LANES)
    qpos = i * qb + lax.broadcasted_iota(jnp.int32, (qb, 1), 0)

    def bias_of(mask):
        return jnp.where(mask, 0.0, NEG_BIG)

    nsel = SEL_TOPN * SEL_BLOCK
    lane_blk = lax.broadcasted_iota(jnp.int32, (1, nsel), 1) // SEL_BLOCK
    lane_off = lax.broadcasted_iota(jnp.int32, (1, nsel), 1) % SEL_BLOCK
    n_sub = qb // SEL_BLOCK
    sel_jobs = []
    for grp in range(NSA_GROUPS):
        for n in range(n_sub):
            slot = grp * n_sub + n
            base = ((bi * NSA_GROUPS + grp) * nsub_total + i * n_sub + n) * SEL_TOPN
            tok = lane_off
            for t in range(SEL_TOPN):
                idx = sel_ref[base + t]
                off = pl.multiple_of(idx * SEL_BLOCK, SEL_BLOCK)
                kg_ref[slot, t * SEL_BLOCK:(t + 1) * SEL_BLOCK, :] = ksel_ref[pl.ds(off, SEL_BLOCK), :]
                vg_ref[slot, t * SEL_BLOCK:(t + 1) * SEL_BLOCK, :] = vsel_ref[pl.ds(off, SEL_BLOCK), :]
                tok = tok + jnp.where(lane_blk == t, idx * SEL_BLOCK, 0)
            qs = qn_ref[grp * NSA_REP:(grp + 1) * NSA_REP, n * SEL_BLOCK:(n + 1) * SEL_BLOCK, :]
            qp = i * qb + n * SEL_BLOCK + lax.broadcasted_iota(jnp.int32, (SEL_BLOCK, 1), 0)
            sel_jobs.append((slot, qs.reshape(NSA_REP * SEL_BLOCK, LANES), bias_of(tok <= qp)))

    c_end = lax.broadcasted_iota(jnp.int32, (1, nch), 1) * CMP_STRIDE + (CMP_LEN - 1)
    wlen = WINDOW + qb
    start = pl.multiple_of(jnp.maximum(i * qb - WINDOW, 0), qb)
    delta = qpos - (start + lax.broadcasted_iota(jnp.int32, (1, wlen), 1))

    jobs = [(lambda: _dot_nt(q, kc_ref[...]), bias_of(c_end <= qpos), n_heads, lambda: vc_ref[...]),
            (lambda: _dot_nt(q, kwin_ref[pl.ds(start, wlen), :]), bias_of((delta >= 0) & (delta < WINDOW)),
             n_heads, lambda: vwin_ref[pl.ds(start, wlen), :])]
    for slot, qs, bias in sel_jobs:
        jobs.append((lambda qs=qs, slot=slot: _dot_nt(qs, kg_ref[slot]), bias, NSA_REP,
                     lambda slot=slot: vg_ref[slot]))
    scores, nums, outs = {}, {}, []
    for step in range(len(jobs) + NSA_LOOKAHEAD + 1):
        if step < len(jobs):
            scores[step] = jobs[step][0]()
        j = step - NSA_LOOKAHEAD
        if 0 <= j < len(jobs):
            nums[j] = _softmax_num(scores.pop(j), jobs[j][1], jobs[j][2])
        j = step - NSA_LOOKAHEAD - 1
        if 0 <= j < len(jobs):
            e, den = nums.pop(j)
            outs.append(_dot(e, jobs[j][3]()) / den)
    any_c = jnp.where(qpos >= CMP_LEN - 1, 1.0, 0.0)
    o_c = (outs[0].reshape(n_heads, qb, LANES) * any_c[None]).reshape(n_heads * qb, LANES)
    o_w = outs[1]
    o_s = [[outs[2 + grp * n_sub + n] for n in range(n_sub)] for grp in range(NSA_GROUPS)]


    lane = lax.broadcasted_iota(jnp.int32, (qb, LANES), 1)
    low = lane < hd
    gates = gate_ref[...]
    mixed = []
    for a in range(n_heads):
        grp, r = a // NSA_REP, a % NSA_REP
        rs = slice(a * qb, (a + 1) * qb)
        sel_rows = jnp.concatenate([o_s[grp][n][r * SEL_BLOCK:(r + 1) * SEL_BLOCK] for n in range(n_sub)],
                                   axis=0)
        mixed.append(gates[:, 3 * a:3 * a + 1] * o_c[rs] + gates[:, 3 * a + 1:3 * a + 2] * sel_rows
                     + gates[:, 3 * a + 2:3 * a + 3] * o_w[rs])
    for pr in range(n_heads // 2):
        grp = (2 * pr) // NSA_REP
        left, right = mixed[2 * pr], mixed[2 * pr + 1]
        if grp == 0:
            right = pltpu.roll(right, hd, 1)
        else:
            left = pltpu.roll(left, hd, 1)
        o_ref[:, pr * LANES:(pr + 1) * LANES] = jnp.where(low, left, right).astype(o_ref.dtype)


def _nsa_attn(sel, qn, kc, vc, ksel, vsel, kwin, vwin, gates):
    b, n_heads, s, _ = qn.shape
    nch = kc.shape[1]
    qb = min(NSA_QB, s)
    assert s >= WINDOW + qb
    n_blk = min(NSA_BLOCKS_PER_STEP, s // qb)
    rows = qb * n_blk
    full = lambda n: pl.BlockSpec((None, n, LANES), lambda bi, i, sel: (bi, 0, 0))
    grid_spec = pltpu.PrefetchScalarGridSpec(
        num_scalar_prefetch=1,
        grid=(b, s // rows),
        in_specs=[pl.BlockSpec((None, n_heads, rows, LANES), lambda bi, i, sel: (bi, 0, i, 0)),
                  full(nch), full(nch), full(s), full(s), full(s), full(s),
                  pl.BlockSpec((None, rows, LANES), lambda bi, i, sel: (bi, i, 0))],
        out_specs=pl.BlockSpec((None, rows, n_heads * HEAD_DIM), lambda bi, i, sel: (bi, i, 0)),
        scratch_shapes=[pltpu.VMEM((n_blk, NSA_GROUPS * (qb // SEL_BLOCK), SEL_TOPN * SEL_BLOCK, LANES),
                                   BF16)] * 2,
    )
    return pl.pallas_call(
        _nsa_kernel,
        grid_spec=grid_spec,
        out_shape=jax.ShapeDtypeStruct((b, s, n_heads * HEAD_DIM), BF16),
        compiler_params=_cparams("parallel", "arbitrary"),
        name="nsa_attn",
    )(sel, qn, kc, vc, ksel, vsel, kwin, vwin, gates)


def _prep_hybrid(w_in, w_out, pe_k, w1_k, w2_k, pe_v, w1_v, w2_v):
    d = w_in.shape[0]
    hd = HEAD_DIM
    col = np.arange(w_in.shape[1])
    o_qn = 3 * SB_W
    o_kv = o_qn + NSA_QW
    kv = lambda n: col[o_kv + n * NSA_KVW:o_kv + (n + 1) * NSA_KVW]
    qn_c = col[o_qn:o_qn + NSA_QW]
    k3_c = np.concatenate([kv(0), kv(2), kv(4)])
    v3_c = np.concatenate([kv(1), kv(3), kv(5)])
    gate_c = col[o_kv + 6 * NSA_KVW:]
    order = np.concatenate([col[:o_qn], qn_c, k3_c, v3_c, gate_c])
    w = jnp.zeros((d, _C_END), BF16).at[:, :order.shape[0]].set(w_in[:, order].astype(BF16))
    lanes = np.arange(LANES)
    inv = (ROPE_THETA ** (-(lanes % ROPE_HALF).astype(np.float32) / ROPE_HALF)).astype(np.float32)[None]
    sgn = np.where(lanes % hd < ROPE_HALF, -1.0, 1.0).astype(np.float32)[None]

    def cmp_w(pe, w1, w2):
        half = w1.shape[0] // 2
        wab = jnp.concatenate([w1[:half], w1[half:]], axis=1).astype(BF16)
        return wab, pe.reshape(1, -1).astype(F32), w1.astype(BF16), w2.astype(BF16)

    return (w, jnp.asarray(inv), jnp.asarray(sgn), w_out[:SB_W].astype(BF16), w_out[SB_W:].astype(BF16),
            cmp_w(pe_k, w1_k, w2_k), cmp_w(pe_v, w1_v, w2_v))


def _hybrid_layer(h, positions, g_pre, g_post, w, inv, sgn, wo_sb, wo_nsa, cmp_k, cmp_v):
    b, s, d = h.shape
    posf = positions.astype(F32)[..., None]
    (qsb, ksb, vsb, qn, kcmp, vcmp, ksel, vsel, kwin, vwin, gates) = _hyb_proj(h, g_pre, posf, inv, sgn, w)
    o_sb = _sb_attn(qsb, ksb, vsb)
    half = CMP_LEN // 2
    chunks = lambda t: t.reshape(b, NSA_GROUPS, s // half, half * HEAD_DIM)
    kc = _compress(chunks(kcmp), *cmp_k)
    vc = _compress(chunks(vcmp), *cmp_v)
    sel = _nsa_select(qn[:, :, ::SEL_BLOCK, :], kc)
    sel = sel[:, :, :SEL_TOPN].reshape(-1)
    o_nsa = _nsa_attn(sel, qn, kc, vc, ksel, vsel, kwin, vwin, gates)
    hf = h.reshape(b * s, d)
    return _outproj([o_sb.reshape(b * s, SB_W), o_nsa.reshape(b * s, NSA_QW)], [wo_sb, wo_nsa], hf,
                    g_post).reshape(b, s, d)


CROSS_HEADS = 4


def kernel(x, mem, positions, norm_g, ffn1_w_gu, ffn1_w_down, ffn2_w_gu, ffn2_w_down, cross_wq, cross_wkv,
           cross_wo, hyb_w_in, hyb_w_out, cmp_pe_k, cmp_w1_k, cmp_w2_k, cmp_pe_v, cmp_w1_v, cmp_w2_v,
           ssd_w_in, ssd_conv_w, ssd_conv_b, ssd_dt_bias, ssd_A_log, ssd_D, ssd_norm_g, ssd_w_out):
    b, s, d = x.shape
    depth = norm_g.shape[0]
    h = x
    for i in range(depth):
        g = norm_g[i].astype(F32)[:, None, :]
        j = i // 2
        h = _ffn(h.reshape(b * s, d), g[0], g[1], ffn1_w_gu[i].astype(BF16),
                 ffn1_w_down[i].astype(BF16)).reshape(b, s, d)
        if i % 2 == 0:
            h = _hybrid_layer(h, positions, g[2], g[3],
                              *_prep_hybrid(hyb_w_in[j], hyb_w_out[j], cmp_pe_k[j], cmp_w1_k[j], cmp_w2_k[j],
                                            cmp_pe_v[j], cmp_w1_v[j], cmp_w2_v[j]))
        else:
            h = _ssd_layer(h, g[2], g[3], *_prep_ssd(ssd_w_in[j], ssd_conv_w[j], ssd_conv_b[j], ssd_dt_bias[j],
                                                     ssd_A_log[j], ssd_D[j], ssd_norm_g[j], ssd_w_out[j]))
        k, v = _memkv(mem, g[6], cross_wkv[i].astype(BF16))
        h = _cross(h, g[4], g[5], cross_wq[i].astype(BF16), k, v, cross_wo[i].astype(BF16), CROSS_HEADS)
        h = _ffn(h.reshape(b * s, d), g[7], g[8], ffn2_w_gu[i].astype(BF16),
                 ffn2_w_down[i].astype(BF16)).reshape(b, s, d)
    return h
```
